```python
import math
import jax, jax.numpy as jnp
from jax import lax
import numpy as np

D_MODEL = 2048
BATCH = 1
SEQ = 8192
DEPTH = 2

N_MIXERS = 2
N_GDN_LAYERS = (DEPTH + 1) // 2
N_HGRN_LAYERS = DEPTH // 2

GDN_K_HEADS = 16
GDN_V_HEADS = 32
GDN_HEAD_K = 128
GDN_HEAD_V = 128
GDN_KEY_DIM = GDN_K_HEADS * GDN_HEAD_K
GDN_VAL_DIM = GDN_V_HEADS * GDN_HEAD_V
GDN_CONV = 4
GDN_CONV_DIM = 2 * GDN_KEY_DIM + GDN_VAL_DIM
GDN_IN_DIM = GDN_CONV_DIM + GDN_VAL_DIM + 2 * GDN_V_HEADS

HGRN_EXPAND = 128
HGRN_HEADS = D_MODEL // HGRN_EXPAND
HGRN_HEAD_V = D_MODEL // HGRN_HEADS
HGRN_KEY_DIM = HGRN_HEADS * HGRN_EXPAND
HGRN_VAL_DIM = HGRN_HEADS * HGRN_HEAD_V
HGRN_IN_DIM = 2 * HGRN_KEY_DIM + 2 * HGRN_VAL_DIM

D_FF = -(-8 * D_MODEL // (3 * 256)) * 256
CHUNK = 64
EPS = 1e-6

kernel_name = "hybrid_gdn_hgrn2_trunk"


def rmsnorm(x, w, eps=EPS):
    xf = x.astype(jnp.float32)
    y = xf * lax.rsqrt(jnp.mean(xf * xf, axis=-1, keepdims=True) + eps)
    return (y * w.astype(jnp.float32)).astype(x.dtype)


def l2norm(x, eps=1e-6):
    xf = x.astype(jnp.float32)
    return xf * lax.rsqrt(jnp.sum(xf * xf, axis=-1, keepdims=True) + eps)


def causal_depthwise_conv(x, w):
    K, C = w.shape
    return lax.conv_general_dilated(
        x, w[:, None, :].astype(x.dtype), window_strides=(1,), padding=[(K - 1, 0)],
        dimension_numbers=("NWC", "WIO", "NWC"), feature_group_count=C)


def to_chunks(t):
    B, S, H, d = t.shape
    return t.transpose(0, 2, 1, 3).reshape(B, H, S // CHUNK, CHUNK, d)


def from_chunks(o):
    N, B, H, C, d = o.shape
    return jnp.moveaxis(o, 0, 2).reshape(B, H, N * C, d).transpose(0, 2, 1, 3)


def gated_delta_rule_chunked(q, k, v, beta, g):
    qc, kc, vc = to_chunks(q), to_chunks(k), to_chunks(v.astype(jnp.float32))
    bc = to_chunks(beta[..., None])[..., 0]
    G = jnp.cumsum(to_chunks(g[..., None])[..., 0], axis=-1)
    causal = jnp.tril(jnp.ones((CHUNK, CHUNK), bool))
    strict = jnp.tril(jnp.ones((CHUNK, CHUNK), bool), k=-1)
    decay = jnp.exp(jnp.where(causal, G[..., :, None] - G[..., None, :], -jnp.inf))
    kb = kc * bc[..., None]
    A = jnp.where(strict, jnp.einsum("bhnid,bhnjd->bhnij", kb, kc) * decay, 0.0)
    eye = jnp.eye(CHUNK, dtype=jnp.float32)
    T = lax.linalg.triangular_solve(eye + A, jnp.broadcast_to(eye, A.shape), left_side=True,
                                    lower=True, unit_diagonal=True)
    u = jnp.einsum("bhnij,bhnje->bhnie", T, vc * bc[..., None])
    w = jnp.einsum("bhnij,bhnjd->bhnid", T, kb * jnp.exp(G)[..., None])
    a_qk = jnp.einsum("bhnid,bhnjd->bhnij", qc, kc) * decay
    q_dec = qc * jnp.exp(G)[..., None]
    k_dec = kc * jnp.exp(G[..., -1:] - G)[..., None]
    g_last = jnp.exp(G[..., -1])

    def step(S, inp):
        u_c, w_c, aqk_c, qd_c, kd_c, gl_c = inp
        v_new = u_c - jnp.einsum("bhid,bhde->bhie", w_c, S)
        o = jnp.einsum("bhid,bhde->bhie", qd_c, S) + jnp.einsum("bhij,bhje->bhie", aqk_c, v_new)
        S = S * gl_c[..., None, None] + jnp.einsum("bhid,bhie->bhde", kd_c, v_new)
        return S, o

    B, H = qc.shape[0], qc.shape[1]
    S0 = jnp.zeros((B, H, qc.shape[-1], vc.shape[-1]), jnp.float32)
    xs = tuple(jnp.moveaxis(t, 2, 0) for t in (u, w, a_qk, q_dec, k_dec, g_last))
    _, o = lax.scan(step, S0, xs)
    return from_chunks(o)


def hgrn2_chunked(q, k, v, logf):
    qc, kc, vc = to_chunks(q), to_chunks(k), to_chunks(v.astype(jnp.float32))
    Bc = jnp.cumsum(to_chunks(logf), axis=-2)
    q_dec = qc * jnp.exp(Bc)
    k_dec = kc * jnp.exp(Bc[..., -1:, :] - Bc)
    f_last = jnp.exp(Bc[..., -1, :])
    causal = jnp.tril(jnp.ones((CHUNK, CHUNK), bool))[:, :, None]

    def step(S, inp):
        q_c, k_c, v_c, b_c, qd_c, kd_c, fl_c = inp
        dec = jnp.exp(jnp.where(causal, b_c[..., :, None, :] - b_c[..., None, :, :], -jnp.inf))
        scores = jnp.einsum("bhid,bhjd,bhijd->bhij", q_c, k_c, dec)
        o = jnp.einsum("bhid,bhde->bhie", qd_c, S) + jnp.einsum("bhij,bhje->bhie", scores, v_c)
        S = S * fl_c[..., :, None] + jnp.einsum("bhjd,bhje->bhde", kd_c, v_c)
        return S, o

    B, H = qc.shape[0], qc.shape[1]
    S0 = jnp.zeros((B, H, qc.shape[-1], vc.shape[-1]), jnp.float32)
    xs = tuple(jnp.moveaxis(t, 2, 0) for t in (qc, kc, vc, Bc, q_dec, k_dec, f_last))
    _, o = lax.scan(step, S0, xs)
    return from_chunks(o)


def gdn_mixer(h, w_in, conv_w, a_log, dt_bias, head_norm, w_out):
    B, S, _ = h.shape
    proj = h @ w_in
    qkv = jax.nn.silu(causal_depthwise_conv(proj[..., :GDN_CONV_DIM], conv_w))
    z = proj[..., GDN_CONV_DIM:GDN_CONV_DIM + GDN_VAL_DIM]
    b = proj[..., GDN_CONV_DIM + GDN_VAL_DIM:GDN_CONV_DIM + GDN_VAL_DIM + GDN_V_HEADS]
    a = proj[..., GDN_CONV_DIM + GDN_VAL_DIM + GDN_V_HEADS:]
    q = l2norm(qkv[..., :GDN_KEY_DIM].reshape(B, S, GDN_K_HEADS, GDN_HEAD_K)) * (GDN_HEAD_K ** -0.5)
    k = l2norm(qkv[..., GDN_KEY_DIM:2 * GDN_KEY_DIM].reshape(B, S, GDN_K_HEADS, GDN_HEAD_K))
    v = qkv[..., 2 * GDN_KEY_DIM:].reshape(B, S, GDN_V_HEADS, GDN_HEAD_V)
    rep = GDN_V_HEADS // GDN_K_HEADS
    q = jnp.repeat(q, rep, axis=2)
    k = jnp.repeat(k, rep, axis=2)
    beta = jax.nn.sigmoid(b.astype(jnp.float32))
    g = -jnp.exp(a_log.astype(jnp.float32)) * jax.nn.softplus(a.astype(jnp.float32) + dt_bias.astype(jnp.float32))
    o = gated_delta_rule_chunked(q, k, v, beta, g).astype(h.dtype)
    o = rmsnorm(o, head_norm) * jax.nn.silu(z.reshape(B, S, GDN_V_HEADS, GDN_HEAD_V))
    return o.reshape(B, S, GDN_VAL_DIM) @ w_out


def hgrn2_mixer(h, w_in, lower_bound, head_norm, w_out):
    B, S, _ = h.shape
    proj = h @ w_in
    q = jax.nn.silu(proj[..., :HGRN_KEY_DIM].astype(jnp.float32)).reshape(B, S, HGRN_HEADS, HGRN_EXPAND)
    fl = proj[..., HGRN_KEY_DIM:2 * HGRN_KEY_DIM].astype(jnp.float32).reshape(B, S, HGRN_HEADS, HGRN_EXPAND)
    i = proj[..., 2 * HGRN_KEY_DIM:2 * HGRN_KEY_DIM + HGRN_VAL_DIM].reshape(B, S, HGRN_HEADS, HGRN_HEAD_V)
    gate = proj[..., 2 * HGRN_KEY_DIM + HGRN_VAL_DIM:].reshape(B, S, HGRN_HEADS, HGRN_HEAD_V)
    lb = lower_bound.astype(jnp.float32).reshape(HGRN_HEADS, HGRN_EXPAND)
    logf = jnp.logaddexp(jnp.log(lb), jnp.log1p(-lb) + jax.nn.log_sigmoid(fl))
    k = (1.0 - lb) * jax.nn.sigmoid(-fl)
    o = hgrn2_chunked(q, k, i, logf).astype(h.dtype)
    o = rmsnorm(o, head_norm) * jax.nn.silu(gate)
    return o.reshape(B, S, HGRN_VAL_DIM) @ w_out


def swiglu(h, w_gate, w_up, w_down):
    return (jax.nn.silu(h @ w_gate) * (h @ w_up)) @ w_down


def setup_inputs(seed: int = 0) -> dict:
    key = jax.random.key(seed)
    ks = jax.random.split(key, 20)
    f32 = jnp.float32

    def nrm(k, shape, scale):
        return jax.random.normal(k, shape, f32) * scale

    def gain(k, shape):
        return 1.0 + 0.02 * jax.random.normal(k, shape, f32)

    dt = jnp.exp(jax.random.uniform(ks[5], (N_GDN_LAYERS, GDN_V_HEADS), f32, math.log(1e-3), math.log(1e-1)))
    return {
        "x": nrm(ks[0], (BATCH, SEQ, D_MODEL), 1.0),
        "gdn_norm": gain(ks[1], (N_GDN_LAYERS, D_MODEL)),
        "gdn_w_in": nrm(ks[2], (N_GDN_LAYERS, D_MODEL, GDN_IN_DIM), D_MODEL ** -0.5),
        "gdn_conv": nrm(ks[3], (N_GDN_LAYERS, GDN_CONV, GDN_CONV_DIM), GDN_CONV ** -0.5),
        "gdn_a_log": jnp.log(jax.random.uniform(ks[4], (N_GDN_LAYERS, GDN_V_HEADS), f32, 1.0, 16.0)),
        "gdn_dt_bias": dt + jnp.log(-jnp.expm1(-dt)),
        "gdn_head_norm": gain(ks[6], (N_GDN_LAYERS, GDN_HEAD_V)),
        "gdn_w_out": nrm(ks[7], (N_GDN_LAYERS, GDN_VAL_DIM, D_MODEL), GDN_VAL_DIM ** -0.5),
        "hgrn_norm": gain(ks[8], (N_HGRN_LAYERS, D_MODEL)),
        "hgrn_w_in": nrm(ks[9], (N_HGRN_LAYERS, D_MODEL, HGRN_IN_DIM), D_MODEL ** -0.5),
        "hgrn_lower_bounds": nrm(ks[10], (DEPTH, HGRN_KEY_DIM), 0.5),
        "hgrn_head_norm": gain(ks[11], (N_HGRN_LAYERS, HGRN_HEAD_V)),
        "hgrn_w_out": nrm(ks[12], (N_HGRN_LAYERS, HGRN_VAL_DIM, D_MODEL), HGRN_VAL_DIM ** -0.5),
        "ffn_norm": gain(ks[13], (DEPTH, D_MODEL)),
        "ffn_w_gate": nrm(ks[14], (DEPTH, D_MODEL, D_FF), D_MODEL ** -0.5),
        "ffn_w_up": nrm(ks[15], (DEPTH, D_MODEL, D_FF), D_MODEL ** -0.5),
        "ffn_w_down": nrm(ks[16], (DEPTH, D_FF, D_MODEL), D_FF ** -0.5),
        "final_norm": gain(ks[17], (D_MODEL,)),
    }


def reference(x, gdn_norm, gdn_w_in, gdn_conv, gdn_a_log, gdn_dt_bias, gdn_head_norm, gdn_w_out,
              hgrn_norm, hgrn_w_in, hgrn_lower_bounds, hgrn_head_norm, hgrn_w_out,
              ffn_norm, ffn_w_gate, ffn_w_up, ffn_w_down, final_norm):
    lb_table = jnp.cumsum(jax.nn.softmax(hgrn_lower_bounds.astype(jnp.float32), axis=0), axis=0)
    lb_table = lb_table - lb_table[0]
    h = x
    for layer in range(DEPTH):
        j = layer // N_MIXERS
        if layer % N_MIXERS == 0:
            h = h + gdn_mixer(rmsnorm(h, gdn_norm[j]), gdn_w_in[j], gdn_conv[j], gdn_a_log[j],
                              gdn_dt_bias[j], gdn_head_norm[j], gdn_w_out[j])
        else:
            h = h + hgrn2_mixer(rmsnorm(h, hgrn_norm[j]), hgrn_w_in[j], lb_table[layer],
                                hgrn_head_norm[j], hgrn_w_out[j])
        h = h + swiglu(rmsnorm(h, ffn_norm[layer]), ffn_w_gate[layer], ffn_w_up[layer], ffn_w_down[layer])
    return rmsnorm(h, final_norm)
```

```python
import functools

import jax
import jax.numpy as jnp
from jax import lax
from jax.experimental import pallas as pl
from jax.experimental.pallas import tpu as pltpu

F32 = jnp.float32
BF16 = jnp.bfloat16

D_MODEL = 2048
SEQ = 8192
CHUNK = 64
HEAD_DIM = 128
EPS = 1e-6

GDN_K_HEADS = 16
GDN_V_HEADS = 32
GDN_KEY_DIM = GDN_K_HEADS * HEAD_DIM
GDN_VAL_DIM = GDN_V_HEADS * HEAD_DIM
GDN_CONV_DIM = 2 * GDN_KEY_DIM + GDN_VAL_DIM
GDN_MAIN_DIM = GDN_CONV_DIM + GDN_VAL_DIM
GDN_CONV_TAPS = 4

HGRN_HEADS = 16
HGRN_DIM = HGRN_HEADS * HEAD_DIM

VMEM_LIMIT_BYTES = 56 * 1024 * 1024
CARRY_ROWS = 8


def _params(*semantics):
    return pltpu.CompilerParams(dimension_semantics=semantics, vmem_limit_bytes=VMEM_LIMIT_BYTES)


def _mm(a, b):
    return jnp.dot(a.astype(BF16), b.astype(BF16), preferred_element_type=F32)


def _mm_nt(a, b):
    return lax.dot_general(a.astype(BF16), b.astype(BF16), (((1,), (1,)), ((), ())),
                           preferred_element_type=F32)


def _mm_tn(a, b):
    return lax.dot_general(a.astype(BF16), b.astype(BF16), (((0,), (0,)), ((), ())),
                           preferred_element_type=F32)


def _sigmoid(x):
    return 1.0 / (1.0 + jnp.exp(-x))


def _silu(x):
    return x * _sigmoid(x)


def _rmsnorm_kernel(x_ref, w_ref, o_ref):
    x = x_ref[...]
    y = x * lax.rsqrt(jnp.mean(x * x, axis=-1, keepdims=True) + EPS)
    o_ref[...] = (y * w_ref[...]).astype(o_ref.dtype)


def _rmsnorm(x, w, out_dtype, rows=512):
    m, d = x.shape
    return pl.pallas_call(
        _rmsnorm_kernel,
        grid=(m // rows,),
        in_specs=[pl.BlockSpec((rows, d), lambda i: (i, 0)),
                  pl.BlockSpec((1, d), lambda i: (0, 0))],
        out_specs=pl.BlockSpec((rows, d), lambda i: (i, 0)),
        out_shape=jax.ShapeDtypeStruct((m, d), out_dtype),
        compiler_params=_params("arbitrary"),
        name="rmsnorm",
    )(x, w.reshape(1, d))


def _proj_kernel(x_ref, w_ref, o_ref, wb_ref):
    @pl.when(pl.program_id(1) == 0)
    def _():
        wb_ref[...] = w_ref[...].astype(BF16)

    o_ref[...] = jnp.dot(x_ref[...], wb_ref[...], preferred_element_type=F32).astype(o_ref.dtype)


def _proj(x, w, n_cols, tm, tn, out_dtype=BF16):
    m, k = x.shape
    return pl.pallas_call(
        _proj_kernel,
        grid=(n_cols // tn, m // tm),
        in_specs=[pl.BlockSpec((tm, k), lambda j, i: (i, 0)),
                  pl.BlockSpec((k, tn), lambda j, i: (0, j))],
        out_specs=pl.BlockSpec((tm, tn), lambda j, i: (i, j)),
        out_shape=jax.ShapeDtypeStruct((m, n_cols), out_dtype),
        scratch_shapes=[pltpu.VMEM((k, tn), BF16)],
        compiler_params=_params("arbitrary", "arbitrary"),
        name="proj",
    )(x, w)


def _proj_residual_kernel(x_ref, w_ref, r_ref, o_ref, wb_ref):
    @pl.when(pl.program_id(1) == 0)
    def _():
        wb_ref[...] = w_ref[...].astype(BF16)

    o_ref[...] = r_ref[...] + jnp.dot(x_ref[...], wb_ref[...], preferred_element_type=F32)


def _proj_residual(x, w, res, tm, tn):
    m, k = x.shape
    n = w.shape[1]
    return pl.pallas_call(
        _proj_residual_kernel,
        grid=(n // tn, m // tm),
        in_specs=[pl.BlockSpec((tm, k), lambda j, i: (i, 0)),
                  pl.BlockSpec((k, tn), lambda j, i: (0, j)),
                  pl.BlockSpec((tm, tn), lambda j, i: (i, j))],
        out_specs=pl.BlockSpec((tm, tn), lambda j, i: (i, j)),
        out_shape=jax.ShapeDtypeStruct((m, n), F32),
        scratch_shapes=[pltpu.VMEM((k, tn), BF16)],
        compiler_params=_params("arbitrary", "arbitrary"),
        name="proj_residual",
    )(x, w, res)


def _swiglu_up_kernel(x_ref, wg_ref, wu_ref, o_ref, wgb_ref, wub_ref):
    @pl.when(pl.program_id(1) == 0)
    def _():
        wgb_ref[...] = wg_ref[...].astype(BF16)
        wub_ref[...] = wu_ref[...].astype(BF16)

    x = x_ref[...]
    g = jnp.dot(x, wgb_ref[...], preferred_element_type=F32)
    u = jnp.dot(x, wub_ref[...], preferred_element_type=F32)
    o_ref[...] = (_silu(g) * u).astype(o_ref.dtype)


def _swiglu_up(x, w_gate, w_up, tm, tn):
    m, k = x.shape
    n = w_gate.shape[1]
    return pl.pallas_call(
        _swiglu_up_kernel,
        grid=(n // tn, m // tm),
        in_specs=[pl.BlockSpec((tm, k), lambda j, i: (i, 0)),
                  pl.BlockSpec((k, tn), lambda j, i: (0, j)),
                  pl.BlockSpec((k, tn), lambda j, i: (0, j))],
        out_specs=pl.BlockSpec((tm, tn), lambda j, i: (i, j)),
        out_shape=jax.ShapeDtypeStruct((m, n), BF16),
        scratch_shapes=[pltpu.VMEM((k, tn), BF16), pltpu.VMEM((k, tn), BF16)],
        compiler_params=_params("arbitrary", "arbitrary"),
        name="swiglu_up",
    )(x, w_gate, w_up)


def _gdn_gate_kernel(x_ref, w_ref, alog_ref, dtb_ref, g_ref, beta_ref):
    ba = _mm_nt(w_ref[...], x_ref[...])
    nh = GDN_V_HEADS
    beta_ref[...] = _sigmoid(ba[:nh])
    a = ba[nh:] + dtb_ref[...]
    softplus = jnp.maximum(a, 0.0) + jnp.log1p(jnp.exp(-jnp.abs(a)))
    g = -jnp.exp(alog_ref[...]) * softplus
    pos = lax.broadcasted_iota(jnp.int32, g.shape, 1) % CHUNK
    shift = 1
    while shift < CHUNK:
        g = g + jnp.where(pos >= shift, pltpu.roll(g, shift, 1), 0.0)
        shift *= 2
    g_ref[...] = g


def _gdn_gates(x, w_ba_t, a_log, dt_bias, rows=512):
    m, k = x.shape
    nh = GDN_V_HEADS
    return pl.pallas_call(
        _gdn_gate_kernel,
        grid=(m // rows,),
        in_specs=[pl.BlockSpec((rows, k), lambda i: (i, 0)),
                  pl.BlockSpec((2 * nh, k), lambda i: (0, 0)),
                  pl.BlockSpec((nh, 1), lambda i: (0, 0)),
                  pl.BlockSpec((nh, 1), lambda i: (0, 0))],
        out_specs=[pl.BlockSpec((nh, rows), lambda i: (0, i)),
                   pl.BlockSpec((nh, rows), lambda i: (0, i))],
        out_shape=[jax.ShapeDtypeStruct((nh, m), F32), jax.ShapeDtypeStruct((nh, m), F32)],
        compiler_params=_params("arbitrary"),
        name="gdn_gates",
    )(x, w_ba_t, a_log.reshape(nh, 1), dt_bias.reshape(nh, 1))


def _conv_silu(x_ref, carry_ref, w_ref):
    x = x_ref[...].astype(F32)
    rows = x.shape[0]
    xe = jnp.concatenate([carry_ref[...], x], axis=0)
    w = w_ref[...]
    taps = GDN_CONV_TAPS
    y = w[taps - 1:taps] * xe
    for j in range(1, taps):
        y = y + w[taps - 1 - j:taps - j] * pltpu.roll(xe, j, 0)
    carry_ref[...] = x[rows - CARRY_ROWS:]
    return _silu(y[CARRY_ROWS:])


def _l2norm(x):
    return x * lax.rsqrt(jnp.sum(x * x, axis=-1, keepdims=True) + 1e-6)


def _gated_rmsnorm(o, w, z):
    y = o * lax.rsqrt(jnp.mean(o * o, axis=-1, keepdims=True) + EPS)
    return y * w * _silu(z)


def _unit_lower_inverse(a, row, col):
    eye = (row == col).astype(F32)
    same16 = (row // 16) == (col // 16)
    ad = jnp.where(same16, a, 0.0)
    p2 = _mm(ad, ad)
    p4 = _mm(p2, p2)
    p8 = _mm(p4, p4)
    t = eye - ad
    t = t + _mm(t, p2)
    t = t + _mm(t, p4)
    t = t + _mm(t, p8)
    off32 = ((row // 32) == (col // 32)) & jnp.logical_not(same16)
    a1 = jnp.where(off32, a, 0.0)
    t = t - _mm(t, _mm(a1, t))
    a2 = jnp.where((row // 32) != (col // 32), a, 0.0)
    t = t - _mm(t, _mm(a2, t))
    return t


def _gdn_kernel(q_ref, k_ref, v_ref, z_ref, wq_ref, wk_ref, wv_ref, grow_ref, brow_ref,
                gcol_ref, bcol_ref, hn_ref, o_ref, s_ref, cq_ref, ck_ref, cv_ref):
    kh = pl.program_id(0)
    t_blk = pl.program_id(1)
    rows = q_ref.shape[0]

    @pl.when(t_blk == 0)
    def _():
        s_ref[...] = jnp.zeros_like(s_ref)
        cq_ref[...] = jnp.zeros_like(cq_ref)
        ck_ref[...] = jnp.zeros_like(ck_ref)
        cv_ref[...] = jnp.zeros_like(cv_ref)

    q = _l2norm(_conv_silu(q_ref, cq_ref, wq_ref)) * (HEAD_DIM ** -0.5)
    k = _l2norm(_conv_silu(k_ref, ck_ref, wk_ref))
    v2 = _conv_silu(v_ref, cv_ref, wv_ref)
    hn = hn_ref[...]

    row = lax.broadcasted_iota(jnp.int32, (CHUNK, CHUNK), 0)
    col = lax.broadcasted_iota(jnp.int32, (CHUNK, CHUNK), 1)
    causal = row >= col
    strict = row > col
    head_lane = lax.broadcasted_iota(jnp.int32, gcol_ref.shape, 1)

    for e in range(2):
        head = 2 * kh + e
        g_row = grow_ref[pl.ds(head, 1), :]
        g_col = jnp.sum(jnp.where(head_lane == head, gcol_ref[...], 0.0), axis=1, keepdims=True)
        b_col = jnp.sum(jnp.where(head_lane == head, bcol_ref[...], 0.0), axis=1, keepdims=True)
        v = v2[:, e * HEAD_DIM:(e + 1) * HEAD_DIM]
        z = z_ref[:, e * HEAD_DIM:(e + 1) * HEAD_DIM].astype(F32)
        state = s_ref[e]
        for c in range(rows // CHUNK):
            sl = slice(c * CHUNK, (c + 1) * CHUNK)
            qc, kc, vc = q[sl], k[sl], v[sl]
            gc = g_col[sl]
            bc = b_col[sl]
            gr = g_row[:, sl]
            g_last = gc[CHUNK - 1:CHUNK]
            decay = jnp.exp(jnp.where(causal, gc - gr, -jnp.inf))
            kq = _mm_nt(jnp.concatenate([qc, kc], axis=0), kc)
            a = jnp.where(strict, bc * kq[CHUNK:] * decay, 0.0)
            aqk = kq[:CHUNK] * decay
            t_inv = _unit_lower_inverse(a, row, col)
            eg = jnp.exp(gc)
            uw = _mm(t_inv, jnp.concatenate([vc * bc, kc * (bc * eg)], axis=1))
            u, w = uw[:, :HEAD_DIM], uw[:, HEAD_DIM:]
            qd = qc * eg
            kd = kc * jnp.exp(g_last - gc)
            ws_qs = _mm(jnp.concatenate([w, qd], axis=0), state)
            v_new = u - ws_qs[:CHUNK]
            o = ws_qs[CHUNK:] + _mm(aqk, v_new)
            state = state * jnp.exp(g_last) + _mm_tn(kd, v_new)
            o_ref[sl, e * HEAD_DIM:(e + 1) * HEAD_DIM] = _gated_rmsnorm(o, hn, z[sl]).astype(o_ref.dtype)
        s_ref[e] = state


def _gdn_mixer(proj, conv_w, g_row, b_row, g_col, b_col, head_norm, rows=256):
    m = proj.shape[0]
    nh = GDN_V_HEADS
    kb = GDN_KEY_DIM // HEAD_DIM
    vb = GDN_CONV_DIM // 2 // (2 * HEAD_DIM)
    zb = GDN_CONV_DIM // (2 * HEAD_DIM)
    assert vb * 2 * HEAD_DIM == 2 * GDN_KEY_DIM
    return pl.pallas_call(
        _gdn_kernel,
        grid=(GDN_K_HEADS, m // rows),
        in_specs=[
            pl.BlockSpec((rows, HEAD_DIM), lambda h, t: (t, h)),
            pl.BlockSpec((rows, HEAD_DIM), lambda h, t: (t, kb + h)),
            pl.BlockSpec((rows, 2 * HEAD_DIM), lambda h, t: (t, vb + h)),
            pl.BlockSpec((rows, 2 * HEAD_DIM), lambda h, t: (t, zb + h)),
            pl.BlockSpec((GDN_CONV_TAPS, HEAD_DIM), lambda h, t: (0, h)),
            pl.BlockSpec((GDN_CONV_TAPS, HEAD_DIM), lambda h, t: (0, kb + h)),
            pl.BlockSpec((GDN_CONV_TAPS, 2 * HEAD_DIM), lambda h, t: (0, vb + h)),
            pl.BlockSpec((nh, rows), lambda h, t: (0, t)),
            pl.BlockSpec((nh, rows), lambda h, t: (0, t)),
            pl.BlockSpec((rows, nh), lambda h, t: (t, 0)),
            pl.BlockSpec((rows, nh), lambda h, t: (t, 0)),
            pl.BlockSpec((1, HEAD_DIM), lambda h, t: (0, 0)),
        ],
        out_specs=pl.BlockSpec((rows, 2 * HEAD_DIM), lambda h, t: (t, h)),
        out_shape=jax.ShapeDtypeStruct((m, GDN_VAL_DIM), BF16),
        scratch_shapes=[
            pltpu.VMEM((2, HEAD_DIM, HEAD_DIM), F32),
            pltpu.VMEM((CARRY_ROWS, HEAD_DIM), F32),
            pltpu.VMEM((CARRY_ROWS, HEAD_DIM), F32),
            pltpu.VMEM((CARRY_ROWS, 2 * HEAD_DIM), F32),
        ],
        compiler_params=_params("arbitrary", "arbitrary"),
        name="gdn_mixer",
    )(proj, proj, proj, proj, conv_w, conv_w, conv_w, g_row, b_row, g_col, b_col,
      head_norm.reshape(1, HEAD_DIM))


def _chunk_row_broadcast(x, first, period, rows):
    parts = [jnp.broadcast_to(x[r:r + 1], (period, x.shape[1])) for r in range(first, rows, period)]
    return jnp.concatenate(parts, axis=0)


def _hgrn_kernel(q_ref, f_ref, i_ref, g_ref, lb_ref, hn_ref, o_ref, s_ref):
    t_blk = pl.program_id(1)
    rows = q_ref.shape[0]

    @pl.when(t_blk == 0)
    def _():
        s_ref[...] = jnp.zeros_like(s_ref)

    lbs = lb_ref[...]
    mx = jnp.max(lbs, axis=0, keepdims=True)
    ex = jnp.exp(lbs - mx)
    sm = ex / jnp.sum(ex, axis=0, keepdims=True)
    lb = (sm[0:1] + sm[1:2]) - sm[0:1]

    q = _silu(q_ref[...].astype(F32))
    fl = f_ref[...].astype(F32)
    v = i_ref[...].astype(F32)
    gate = g_ref[...].astype(F32)
    en = jnp.exp(-jnp.abs(fl))
    rc = 1.0 / (1.0 + en)
    pos = fl >= 0.0
    sig = jnp.where(pos, rc, en * rc)
    nsig = jnp.where(pos, en * rc, rc)
    logf = jnp.log(lb + (1.0 - lb) * sig)
    kk = (1.0 - lb) * nsig

    rowi = lax.broadcasted_iota(jnp.int32, (rows, HEAD_DIM), 0)
    pos_in_chunk = rowi % CHUNK
    b = logf
    shift = 1
    while shift < CHUNK:
        b = b + jnp.where(pos_in_chunk >= shift, pltpu.roll(b, shift, 0), 0.0)
        shift *= 2
    b_last = _chunk_row_broadcast(b, CHUNK - 1, CHUNK, rows)
    qd = q * jnp.exp(b)
    kd = kk * jnp.exp(b_last - b)

    levels = []
    for s in (32, 16, 8):
        b_mid = _chunk_row_broadcast(b, s - 1, 2 * s, rows)
        lower = (rowi % (2 * s)) >= s
        levels.append((s, jnp.where(lower, q, kk) * jnp.exp(-jnp.abs(b - b_mid))))

    diags = [jnp.sum(q * kk, axis=1, keepdims=True)]
    for d in range(1, 8):
        e = jnp.exp(jnp.minimum(b - pltpu.roll(b, d, 0), 0.0))
        diags.append(jnp.sum(q * pltpu.roll(kk, d, 0) * e, axis=1, keepdims=True))

    row = lax.broadcasted_iota(jnp.int32, (CHUNK, CHUNK), 0)
    col = lax.broadcasted_iota(jnp.int32, (CHUNK, CHUNK), 1)
    hn = hn_ref[...]
    state_t = s_ref[...]
    for c in range(rows // CHUNK):
        sl = slice(c * CHUNK, (c + 1) * CHUNK)
        scores = jnp.zeros((CHUNK, CHUNK), F32)
        for s, x in levels:
            pair = ((row // (2 * s)) == (col // (2 * s))) & ((row % (2 * s)) >= s) & ((col % (2 * s)) < s)
            scores = jnp.where(pair, _mm_nt(x[sl], x[sl]), scores)
        same8 = (row // 8) == (col // 8)
        for d in range(8):
            scores = jnp.where(same8 & (row - col == d), diags[d][sl], scores)
        o = _mm_nt(qd[sl], state_t) + _mm(scores, v[sl])
        f_last = jnp.exp(b[c * CHUNK + CHUNK - 1:(c + 1) * CHUNK])
        state_t = state_t * f_last + _mm_tn(v[sl], kd[sl])
        o_ref[sl, :] = _gated_rmsnorm(o, hn, gate[sl]).astype(o_ref.dtype)
    s_ref[...] = state_t


def _hgrn_mixer(proj, lower_bounds, head_norm, rows=256):
    m = proj.shape[0]
    nb = HGRN_DIM // HEAD_DIM
    return pl.pallas_call(
        _hgrn_kernel,
        grid=(HGRN_HEADS, m // rows),
        in_specs=[
            pl.BlockSpec((rows, HEAD_DIM), lambda h, t: (t, h)),
            pl.BlockSpec((rows, HEAD_DIM), lambda h, t: (t, nb + h)),
            pl.BlockSpec((rows, HEAD_DIM), lambda h, t: (t, 2 * nb + h)),
            pl.BlockSpec((rows, HEAD_DIM), lambda h, t: (t, 3 * nb + h)),
            pl.BlockSpec((2, HEAD_DIM), lambda h, t: (0, h)),
            pl.BlockSpec((1, HEAD_DIM), lambda h, t: (0, 0)),
        ],
        out_specs=pl.BlockSpec((rows, HEAD_DIM), lambda h, t: (t, h)),
        out_shape=jax.ShapeDtypeStruct((m, HGRN_DIM), BF16),
        scratch_shapes=[pltpu.VMEM((HEAD_DIM, HEAD_DIM), F32)],
        compiler_params=_params("arbitrary", "arbitrary"),
        name="hgrn_mixer",
    )(proj, proj, proj, proj, lower_bounds, head_norm.reshape(1, HEAD_DIM))


def _ffn(h, norm_w, w_gate, w_up, w_down):
    hn = _rmsnorm(h, norm_w, BF16)
    act = _swiglu_up(hn, w_gate, w_up, tm=1024, tn=512)
    return _proj_residual(act, w_down, h, tm=512, tn=512)


def kernel(x, gdn_norm, gdn_w_in, gdn_conv, gdn_a_log, gdn_dt_bias, gdn_head_norm, gdn_w_out,
           hgrn_norm, hgrn_w_in, hgrn_lower_bounds, hgrn_head_norm, hgrn_w_out,
           ffn_norm, ffn_w_gate, ffn_w_up, ffn_w_down, final_norm):
    assert x.shape == (1, SEQ, D_MODEL)
    h = x.reshape(SEQ, D_MODEL)

    hn = _rmsnorm(h, gdn_norm[0], BF16)
    proj = _proj(hn, gdn_w_in[0], GDN_MAIN_DIM, tm=1024, tn=1024)
    w_ba_t = gdn_w_in[0][:, GDN_MAIN_DIM:].T
    g_row, b_row = _gdn_gates(hn, w_ba_t, gdn_a_log[0], gdn_dt_bias[0])
    mixed = _gdn_mixer(proj, gdn_conv[0], g_row, b_row, g_row.T, b_row.T, gdn_head_norm[0])
    h = _proj_residual(mixed, gdn_w_out[0], h, tm=1024, tn=512)
    h = _ffn(h, ffn_norm[0], ffn_w_gate[0], ffn_w_up[0], ffn_w_down[0])

    hn = _rmsnorm(h, hgrn_norm[0], BF16)
    proj = _proj(hn, hgrn_w_in[0], 4 * HGRN_DIM, tm=1024, tn=1024)
    mixed = _hgrn_mixer(proj, hgrn_lower_bounds, hgrn_head_norm[0])
    h = _proj_residual(mixed, hgrn_w_out[0], h, tm=1024, tn=512)
    h = _ffn(h, ffn_norm[1], ffn_w_gate[1], ffn_w_up[1], ffn_w_down[1])

    return _rmsnorm(h, final_norm, F32).reshape(1, SEQ, D_MODEL)
```

```python
import jax
import jax.numpy as jnp
from jax import lax
from jax.experimental import pallas as pl
from jax.experimental.pallas import tpu as pltpu

F32 = jnp.float32
BF16 = jnp.bfloat16

D_MODEL = 2048
SEQ = 8192
CHUNK = 64
HEAD_DIM = 128
EPS = 1e-6

GDN_K_HEADS = 16
GDN_V_HEADS = 32
GDN_KEY_DIM = GDN_K_HEADS * HEAD_DIM
GDN_VAL_DIM = GDN_V_HEADS * HEAD_DIM
GDN_CONV_DIM = 2 * GDN_KEY_DIM + GDN_VAL_DIM
GDN_MAIN_DIM = GDN_CONV_DIM + GDN_VAL_DIM
GDN_CONV_TAPS = 4

HGRN_HEADS = 16
HGRN_DIM = HGRN_HEADS * HEAD_DIM

VMEM_LIMIT_BYTES = 56 * 1024 * 1024
CARRY_ROWS = 8

GROUP = 2 * CHUNK
GDN_KH_PER_STEP = 4
GDN_ROWS = 256
HGRN_ROWS = 512


def _params(*semantics):
    return pltpu.CompilerParams(dimension_semantics=semantics, vmem_limit_bytes=VMEM_LIMIT_BYTES)


def _mm(a, b):
    return jnp.dot(a.astype(BF16), b.astype(BF16), preferred_element_type=F32)


def _mm_nt(a, b):
    return lax.dot_general(a.astype(BF16), b.astype(BF16), (((1,), (1,)), ((), ())),
                           preferred_element_type=F32)


def _mm_tn(a, b):
    return lax.dot_general(a.astype(BF16), b.astype(BF16), (((0,), (0,)), ((), ())),
                           preferred_element_type=F32)


def _sigmoid(x):
    return 1.0 / (1.0 + jnp.exp(-x))


def _silu(x):
    return x * _sigmoid(x)


def _rmsnorm_kernel(x_ref, w_ref, o_ref):
    x = x_ref[...]
    y = x * lax.rsqrt(jnp.mean(x * x, axis=-1, keepdims=True) + EPS)
    o_ref[...] = (y * w_ref[...]).astype(o_ref.dtype)


def _rmsnorm(x, w, out_dtype, rows=512):
    m, d = x.shape
    return pl.pallas_call(
        _rmsnorm_kernel,
        grid=(m // rows,),
        in_specs=[pl.BlockSpec((rows, d), lambda i: (i, 0)),
                  pl.BlockSpec((1, d), lambda i: (0, 0))],
        out_specs=pl.BlockSpec((rows, d), lambda i: (i, 0)),
        out_shape=jax.ShapeDtypeStruct((m, d), out_dtype),
        compiler_params=_params("arbitrary"),
        name="rmsnorm",
    )(x, w.reshape(1, d))


def _weight_spec(k, tn, layer):
    return pl.BlockSpec((None, k, tn), lambda j, i: (layer, 0, j))


def _proj_kernel(x_ref, w_ref, o_ref, wb_ref):
    @pl.when(pl.program_id(1) == 0)
    def _():
        wb_ref[...] = w_ref[...].astype(BF16)

    o_ref[...] = jnp.dot(x_ref[...], wb_ref[...], preferred_element_type=F32).astype(o_ref.dtype)


def _proj(x, w, layer, n_cols, tm, tn):
    m, k = x.shape
    return pl.pallas_call(
        _proj_kernel,
        grid=(n_cols // tn, m // tm),
        in_specs=[pl.BlockSpec((tm, k), lambda j, i: (i, 0)), _weight_spec(k, tn, layer)],
        out_specs=pl.BlockSpec((tm, tn), lambda j, i: (i, j)),
        out_shape=jax.ShapeDtypeStruct((m, n_cols), BF16),
        scratch_shapes=[pltpu.VMEM((k, tn), BF16)],
        compiler_params=_params("arbitrary", "arbitrary"),
        name="proj",
    )(x, w)


def _proj_residual_kernel(x_ref, w_ref, r_ref, o_ref, wb_ref):
    @pl.when(pl.program_id(1) == 0)
    def _():
        wb_ref[...] = w_ref[...].astype(BF16)

    o_ref[...] = r_ref[...] + jnp.dot(x_ref[...], wb_ref[...], preferred_element_type=F32)


def _proj_residual(x, w, layer, res, tm, tn):
    m, k = x.shape
    n = w.shape[2]
    return pl.pallas_call(
        _proj_residual_kernel,
        grid=(n // tn, m // tm),
        in_specs=[pl.BlockSpec((tm, k), lambda j, i: (i, 0)), _weight_spec(k, tn, layer),
                  pl.BlockSpec((tm, tn), lambda j, i: (i, j))],
        out_specs=pl.BlockSpec((tm, tn), lambda j, i: (i, j)),
        out_shape=jax.ShapeDtypeStruct((m, n), F32),
        scratch_shapes=[pltpu.VMEM((k, tn), BF16)],
        compiler_params=_params("arbitrary", "arbitrary"),
        name="proj_residual",
    )(x, w, res)


def _swiglu_up_kernel(x_ref, wg_ref, wu_ref, o_ref, wgb_ref, wub_ref):
    @pl.when(pl.program_id(1) == 0)
    def _():
        wgb_ref[...] = wg_ref[...].astype(BF16)
        wub_ref[...] = wu_ref[...].astype(BF16)

    x = x_ref[...]
    g = jnp.dot(x, wgb_ref[...], preferred_element_type=F32)
    u = jnp.dot(x, wub_ref[...], preferred_element_type=F32)
    o_ref[...] = (_silu(g) * u).astype(o_ref.dtype)


def _swiglu_up(x, w_gate, w_up, layer, tm, tn):
    m, k = x.shape
    n = w_gate.shape[2]
    return pl.pallas_call(
        _swiglu_up_kernel,
        grid=(n // tn, m // tm),
        in_specs=[pl.BlockSpec((tm, k), lambda j, i: (i, 0)), _weight_spec(k, tn, layer),
                  _weight_spec(k, tn, layer)],
        out_specs=pl.BlockSpec((tm, tn), lambda j, i: (i, j)),
        out_shape=jax.ShapeDtypeStruct((m, n), BF16),
        scratch_shapes=[pltpu.VMEM((k, tn), BF16), pltpu.VMEM((k, tn), BF16)],
        compiler_params=_params("arbitrary", "arbitrary"),
        name="swiglu_up",
    )(x, w_gate, w_up)


def _gdn_gate_kernel(x_ref, w_ref, alog_ref, dtb_ref, g_ref, beta_ref):
    ba = _mm_nt(w_ref[...], x_ref[...])
    nh = GDN_V_HEADS
    beta_ref[...] = _sigmoid(ba[:nh])
    a = ba[nh:] + dtb_ref[...]
    softplus = jnp.maximum(a, 0.0) + jnp.log1p(jnp.exp(-jnp.abs(a)))
    g = -jnp.exp(alog_ref[...]) * softplus
    pos = lax.broadcasted_iota(jnp.int32, g.shape, 1) % CHUNK
    shift = 1
    while shift < CHUNK:
        g = g + jnp.where(pos >= shift, pltpu.roll(g, shift, 1), 0.0)
        shift *= 2
    g_ref[...] = g


def _gdn_gates(x, w_ba_t, a_log, dt_bias, rows=512):
    m, k = x.shape
    nh = GDN_V_HEADS
    return pl.pallas_call(
        _gdn_gate_kernel,
        grid=(m // rows,),
        in_specs=[pl.BlockSpec((rows, k), lambda i: (i, 0)),
                  pl.BlockSpec((2 * nh, k), lambda i: (0, 0)),
                  pl.BlockSpec((nh, 1), lambda i: (0, 0)),
                  pl.BlockSpec((nh, 1), lambda i: (0, 0))],
        out_specs=[pl.BlockSpec((nh, rows), lambda i: (0, i)),
                   pl.BlockSpec((nh, rows), lambda i: (0, i))],
        out_shape=[jax.ShapeDtypeStruct((nh, m), F32), jax.ShapeDtypeStruct((nh, m), F32)],
        compiler_params=_params("arbitrary"),
        name="gdn_gates",
    )(x, w_ba_t, a_log.reshape(nh, 1), dt_bias.reshape(nh, 1))


def _conv_silu(x_ref, carry_ref, w_ref):
    x = x_ref[...].astype(F32)
    rows = x.shape[0]
    xe = jnp.concatenate([carry_ref[...], x], axis=0)
    w = w_ref[...]
    taps = GDN_CONV_TAPS
    y = w[taps - 1:taps] * xe
    for j in range(1, taps):
        y = y + w[taps - 1 - j:taps - j] * pltpu.roll(xe, j, 0)
    carry_ref[...] = x[rows - CARRY_ROWS:]
    return _silu(y[CARRY_ROWS:])


def _l2norm(x):
    return x * lax.rsqrt(jnp.sum(x * x, axis=-1, keepdims=True) + 1e-6)


def _gated_rmsnorm(o, w, z):
    y = o * lax.rsqrt(jnp.mean(o * o, axis=-1, keepdims=True) + EPS)
    return y * w * _silu(z)


def _unit_lower_inverses(a_list, row, col):
    eye = (row == col).astype(F32)
    same16 = (row // 16) == (col // 16)
    same32 = (row // 32) == (col // 32)
    off32 = same32 & jnp.logical_not(same16)
    off64 = jnp.logical_not(same32)
    ad = [jnp.where(same16, a, 0.0) for a in a_list]
    p2 = [_mm(x, x) for x in ad]
    t = [eye - x for x in ad]
    t = [x + _mm(x, p) for x, p in zip(t, p2)]
    p4 = [_mm(p, p) for p in p2]
    t = [x + _mm(x, p) for x, p in zip(t, p4)]
    p8 = [_mm(p, p) for p in p4]
    t = [x + _mm(x, p) for x, p in zip(t, p8)]
    for mask in (off32, off64):
        y = [_mm(jnp.where(mask, a, 0.0), x) for a, x in zip(a_list, t)]
        t = [x - _mm(x, yy) for x, yy in zip(t, y)]
    return t


def _gdn_kernel(q_ref, k_ref, v_ref, z_ref, wq_ref, wk_ref, wv_ref, grow_ref,
                gcol_ref, bcol_ref, hn_ref, o_ref, s_ref, cq_ref, ck_ref, cv_ref):
    hb = pl.program_id(0)
    rows = q_ref.shape[0]
    n_groups = rows // GROUP
    n_chunks = rows // CHUNK
    per_group = GROUP // CHUNK

    @pl.when(pl.program_id(1) == 0)
    def _():
        s_ref[...] = jnp.zeros_like(s_ref)
        cq_ref[...] = jnp.zeros_like(cq_ref)
        ck_ref[...] = jnp.zeros_like(ck_ref)
        cv_ref[...] = jnp.zeros_like(cv_ref)

    q_all = _conv_silu(q_ref, cq_ref, wq_ref)
    k_all = _conv_silu(k_ref, ck_ref, wk_ref)
    v_all = _conv_silu(v_ref, cv_ref, wv_ref)
    hn = hn_ref[...]

    def lanes(x, i):
        return x[:, i * HEAD_DIM:(i + 1) * HEAD_DIM]

    qs = [_l2norm(lanes(q_all, j)) * (HEAD_DIM ** -0.5) for j in range(GDN_KH_PER_STEP)]
    ks = [_l2norm(lanes(k_all, j)) for j in range(GDN_KH_PER_STEP)]

    row = lax.broadcasted_iota(jnp.int32, (GROUP, GROUP), 0)
    col = lax.broadcasted_iota(jnp.int32, (GROUP, GROUP), 1)
    same_chunk = (row // CHUNK) == (col // CHUNK)
    causal = (row >= col) & same_chunk
    strict = row > col

    heads = [(j, e) for j in range(GDN_KH_PER_STEP) for e in range(2)]
    head_lane = lax.broadcasted_iota(jnp.int32, gcol_ref.shape, 1)
    gcol_all = gcol_ref[...]
    bcol_all = bcol_ref[...]
    g_col, b_col, g_row = [], [], []
    for j, e in heads:
        head = 2 * (GDN_KH_PER_STEP * hb + j) + e
        sel = head_lane == head
        g_col.append(jnp.sum(jnp.where(sel, gcol_all, 0.0), axis=1, keepdims=True))
        b_col.append(jnp.sum(jnp.where(sel, bcol_all, 0.0), axis=1, keepdims=True))
        g_row.append(grow_ref[pl.ds(head, 1), :])

    def grp(x, g):
        return x[g * GROUP:(g + 1) * GROUP]

    kq = [[_mm_nt(jnp.concatenate([grp(qs[j], g), grp(ks[j], g)], axis=0), grp(ks[j], g))
           for g in range(n_groups)] for j in range(GDN_KH_PER_STEP)]

    items = [(h, g) for h in range(len(heads)) for g in range(n_groups)]
    a_list, aqk_list, rhs_list, qd_list, kd_list = [], [], [], [], []
    for h, g in items:
        j, e = heads[h]
        gc = grp(g_col[h], g)
        bc = grp(b_col[h], g)
        gr = g_row[h][:, g * GROUP:(g + 1) * GROUP]
        decay = jnp.exp(jnp.where(causal, gc - gr, -jnp.inf))
        a_list.append(jnp.where(strict, bc * kq[j][g][GROUP:] * decay, 0.0))
        aqk_list.append(kq[j][g][:GROUP] * decay)
        eg = jnp.exp(gc)
        kg = grp(ks[j], g)
        vg = grp(lanes(v_all, 2 * j + e), g)
        rhs_list.append(jnp.concatenate([vg * bc, kg * (bc * eg)], axis=1))
        g_last = jnp.concatenate(
            [jnp.broadcast_to(gc[(i + 1) * CHUNK - 1:(i + 1) * CHUNK], (CHUNK, 1)) for i in range(per_group)],
            axis=0)
        qd_list.append(grp(qs[j], g) * eg)
        kd_list.append(kg * jnp.exp(g_last - gc))

    t_list = _unit_lower_inverses(a_list, row, col)
    uw_list = [_mm(t, r) for t, r in zip(t_list, rhs_list)]

    states = [s_ref[h] for h in range(len(heads))]
    for c in range(n_chunks):
        g, i = divmod(c, per_group)
        sl = slice(i * CHUNK, (i + 1) * CHUNK)
        idx = [h * n_groups + g for h in range(len(heads))]
        ws_qs = [_mm(jnp.concatenate([uw_list[n][sl, HEAD_DIM:], qd_list[n][sl]], axis=0), states[h])
                 for h, n in enumerate(idx)]
        v_new = [uw_list[n][sl, :HEAD_DIM] - r[:CHUNK] for n, r in zip(idx, ws_qs)]
        intra = [_mm(aqk_list[n][sl, i * CHUNK:(i + 1) * CHUNK], vn) for n, vn in zip(idx, v_new)]
        upd = [_mm_tn(kd_list[n][sl], vn) for n, vn in zip(idx, v_new)]
        for h in range(len(heads)):
            j, e = heads[h]
            g_last = g_col[h][(c + 1) * CHUNK - 1:(c + 1) * CHUNK]
            states[h] = states[h] * jnp.exp(g_last) + upd[h]
            o = ws_qs[h][CHUNK:] + intra[h]
            z = z_ref[c * CHUNK:(c + 1) * CHUNK, (2 * j + e) * HEAD_DIM:(2 * j + e + 1) * HEAD_DIM].astype(F32)
            o_ref[c * CHUNK:(c + 1) * CHUNK, (2 * j + e) * HEAD_DIM:(2 * j + e + 1) * HEAD_DIM] = (
                _gated_rmsnorm(o, hn, z).astype(o_ref.dtype))
    for h in range(len(heads)):
        s_ref[h] = states[h]


def _gdn_mixer(proj, conv_w, g_row, g_col, b_col, head_norm):
    m = proj.shape[0]
    rows = GDN_ROWS
    nh = GDN_V_HEADS
    kw = GDN_KH_PER_STEP * HEAD_DIM
    vw = 2 * kw
    kb = GDN_KEY_DIM // kw
    vb = 2 * GDN_KEY_DIM // vw
    zb = GDN_CONV_DIM // vw
    return pl.pallas_call(
        _gdn_kernel,
        grid=(GDN_K_HEADS // GDN_KH_PER_STEP, m // rows),
        in_specs=[
            pl.BlockSpec((rows, kw), lambda h, t: (t, h)),
            pl.BlockSpec((rows, kw), lambda h, t: (t, kb + h)),
            pl.BlockSpec((rows, vw), lambda h, t: (t, vb + h)),
            pl.BlockSpec((rows, vw), lambda h, t: (t, zb + h)),
            pl.BlockSpec((GDN_CONV_TAPS, kw), lambda h, t: (0, h)),
            pl.BlockSpec((GDN_CONV_TAPS, kw), lambda h, t: (0, kb + h)),
            pl.BlockSpec((GDN_CONV_TAPS, vw), lambda h, t: (0, vb + h)),
            pl.BlockSpec((nh, rows), lambda h, t: (0, t)),
            pl.BlockSpec((rows, nh), lambda h, t: (t, 0)),
            pl.BlockSpec((rows, nh), lambda h, t: (t, 0)),
            pl.BlockSpec((1, HEAD_DIM), lambda h, t: (0, 0)),
        ],
        out_specs=pl.BlockSpec((rows, vw), lambda h, t: (t, h)),
        out_shape=jax.ShapeDtypeStruct((m, GDN_VAL_DIM), BF16),
        scratch_shapes=[
            pltpu.VMEM((2 * GDN_KH_PER_STEP, HEAD_DIM, HEAD_DIM), F32),
            pltpu.VMEM((CARRY_ROWS, kw), F32),
            pltpu.VMEM((CARRY_ROWS, kw), F32),
            pltpu.VMEM((CARRY_ROWS, vw), F32),
        ],
        compiler_params=_params("arbitrary", "arbitrary"),
        name="gdn_mixer",
    )(proj, proj, proj, proj, conv_w, conv_w, conv_w, g_row, g_col, b_col,
      head_norm.reshape(1, HEAD_DIM))


def _row_broadcast(x, first, period, rows):
    parts = [jnp.broadcast_to(x[r:r + 1], (period, x.shape[1])) for r in range(first, rows, period)]
    return jnp.concatenate(parts, axis=0)


def _hgrn_kernel(q_ref, f_ref, i_ref, g_ref, lb_ref, hn_ref, o_ref, s_ref):
    rows = q_ref.shape[0]
    n_groups = rows // GROUP
    n_chunks = rows // CHUNK
    per_group = GROUP // CHUNK

    @pl.when(pl.program_id(1) == 0)
    def _():
        s_ref[...] = jnp.zeros_like(s_ref)

    lbs = lb_ref[...]
    mx = jnp.max(lbs, axis=0, keepdims=True)
    ex = jnp.exp(lbs - mx)
    sm = ex / jnp.sum(ex, axis=0, keepdims=True)
    lb = (sm[0:1] + sm[1:2]) - sm[0:1]

    q = _silu(q_ref[...].astype(F32))
    fl = f_ref[...].astype(F32)
    v = i_ref[...].astype(F32)
    en = jnp.exp(-jnp.abs(fl))
    rc = 1.0 / (1.0 + en)
    pos = fl >= 0.0
    sig = jnp.where(pos, rc, en * rc)
    nsig = jnp.where(pos, en * rc, rc)
    logf = jnp.log(lb + (1.0 - lb) * sig)
    kk = (1.0 - lb) * nsig

    rowi = lax.broadcasted_iota(jnp.int32, (rows, HEAD_DIM), 0)
    pos_in_chunk = rowi % CHUNK
    b = logf
    shift = 1
    while shift < CHUNK:
        b = b + jnp.where(pos_in_chunk >= shift, pltpu.roll(b, shift, 0), 0.0)
        shift *= 2
    b_last = _row_broadcast(b, CHUNK - 1, CHUNK, rows)
    qd = q * jnp.exp(b)
    kd = kk * jnp.exp(b_last - b)

    halves = (32, 16, 8)
    level_x = []
    for s in halves:
        b_mid = _row_broadcast(b, s - 1, 2 * s, rows)
        lower = (rowi % (2 * s)) >= s
        level_x.append(jnp.where(lower, q, kk) * jnp.exp(-jnp.abs(b - b_mid)))

    diags = [jnp.sum(q * kk, axis=1, keepdims=True)]
    for d in range(1, 8):
        e = jnp.exp(jnp.minimum(b - pltpu.roll(b, d, 0), 0.0))
        diags.append(jnp.sum(q * pltpu.roll(kk, d, 0) * e, axis=1, keepdims=True))

    def grp(x, g):
        return x[g * GROUP:(g + 1) * GROUP]

    def chk(x, c):
        return x[c * CHUNK:(c + 1) * CHUNK]

    level_mm = [[_mm_nt(grp(x, g), grp(x, g)) for x in level_x] for g in range(n_groups)]
    kv = [_mm_tn(chk(v, c), chk(kd, c)) for c in range(n_chunks)]

    row = lax.broadcasted_iota(jnp.int32, (GROUP, GROUP), 0)
    col = lax.broadcasted_iota(jnp.int32, (GROUP, GROUP), 1)
    scores = []
    for g in range(n_groups):
        sc = jnp.zeros((GROUP, GROUP), F32)
        for s, mm in zip(halves, level_mm[g]):
            pair = ((row // (2 * s)) == (col // (2 * s))) & ((row % (2 * s)) >= s) & ((col % (2 * s)) < s)
            sc = jnp.where(pair, mm, sc)
        same8 = (row // 8) == (col // 8)
        for d in range(8):
            sc = jnp.where(same8 & (row - col == d), grp(diags[d], g), sc)
        scores.append(sc)

    state_t = s_ref[...]
    states = []
    for c in range(n_chunks):
        states.append(state_t)
        f_last = jnp.exp(b[(c + 1) * CHUNK - 1:(c + 1) * CHUNK])
        state_t = state_t * f_last + kv[c]
    s_ref[...] = state_t

    intra = [_mm(scores[g], grp(v, g)) for g in range(n_groups)]
    inter = [_mm_nt(chk(qd, c), states[c]) for c in range(n_chunks)]
    hn = hn_ref[...]
    for c in range(n_chunks):
        g, i = divmod(c, per_group)
        o = inter[c] + intra[g][i * CHUNK:(i + 1) * CHUNK]
        gate = g_ref[c * CHUNK:(c + 1) * CHUNK, :].astype(F32)
        o_ref[c * CHUNK:(c + 1) * CHUNK, :] = _gated_rmsnorm(o, hn, gate).astype(o_ref.dtype)


def _hgrn_mixer(proj, lower_bounds, head_norm):
    m = proj.shape[0]
    rows = HGRN_ROWS
    nb = HGRN_DIM // HEAD_DIM
    return pl.pallas_call(
        _hgrn_kernel,
        grid=(HGRN_HEADS, m // rows),
        in_specs=[
            pl.BlockSpec((rows, HEAD_DIM), lambda h, t: (t, h)),
            pl.BlockSpec((rows, HEAD_DIM), lambda h, t: (t, nb + h)),
            pl.BlockSpec((rows, HEAD_DIM), lambda h, t: (t, 2 * nb + h)),
            pl.BlockSpec((rows, HEAD_DIM), lambda h, t: (t, 3 * nb + h)),
            pl.BlockSpec((2, HEAD_DIM), lambda h, t: (0, h)),
            pl.BlockSpec((1, HEAD_DIM), lambda h, t: (0, 0)),
        ],
        out_specs=pl.BlockSpec((rows, HEAD_DIM), lambda h, t: (t, h)),
        out_shape=jax.ShapeDtypeStruct((m, HGRN_DIM), BF16),
        scratch_shapes=[pltpu.VMEM((HEAD_DIM, HEAD_DIM), F32)],
        compiler_params=_params("arbitrary", "arbitrary"),
        name="hgrn_mixer",
    )(proj, proj, proj, proj, lower_bounds, head_norm.reshape(1, HEAD_DIM))


def _ffn(h, layer, norm_w, w_gate, w_up, w_down):
    hn = _rmsnorm(h, norm_w[layer], BF16)
    act = _swiglu_up(hn, w_gate, w_up, layer, tm=1024, tn=512)
    return _proj_residual(act, w_down, layer, h, tm=512, tn=512)


def kernel(x, gdn_norm, gdn_w_in, gdn_conv, gdn_a_log, gdn_dt_bias, gdn_head_norm, gdn_w_out,
           hgrn_norm, hgrn_w_in, hgrn_lower_bounds, hgrn_head_norm, hgrn_w_out,
           ffn_norm, ffn_w_gate, ffn_w_up, ffn_w_down, final_norm):
    assert x.shape == (1, SEQ, D_MODEL)
    h = x.reshape(SEQ, D_MODEL)

    hn = _rmsnorm(h, gdn_norm[0], BF16)
    proj = _proj(hn, gdn_w_in, 0, GDN_MAIN_DIM, tm=1024, tn=1024)
    w_ba_t = gdn_w_in[0][:, GDN_MAIN_DIM:].T
    g_row, b_row = _gdn_gates(hn, w_ba_t, gdn_a_log[0], gdn_dt_bias[0])
    mixed = _gdn_mixer(proj, gdn_conv[0], g_row, g_row.T, b_row.T, gdn_head_norm[0])
    h = _proj_residual(mixed, gdn_w_out, 0, h, tm=1024, tn=512)
    h = _ffn(h, 0, ffn_norm, ffn_w_gate, ffn_w_up, ffn_w_down)

    hn = _rmsnorm(h, hgrn_norm[0], BF16)
    proj = _proj(hn, hgrn_w_in, 0, 4 * HGRN_DIM, tm=1024, tn=1024)
    mixed = _hgrn_mixer(proj, hgrn_lower_bounds, hgrn_head_norm[0])
    h = _proj_residual(mixed, hgrn_w_out, 0, h, tm=1024, tn=512)
    h = _ffn(h, 1, ffn_norm, ffn_w_gate, ffn_w_up, ffn_w_down)

    return _rmsnorm(h, final_norm, F32).reshape(1, SEQ, D_MODEL)
```

```python
import jax
import jax.numpy as jnp
from jax import lax
from jax.experimental import pallas as pl
from jax.experimental.pallas import tpu as pltpu

F32 = jnp.float32
BF16 = jnp.bfloat16

D_MODEL = 2048
SEQ = 8192
CHUNK = 64
HEAD_DIM = 128
EPS = 1e-6
LOG2E = 1.4426950408889634

GDN_K_HEADS = 16
GDN_V_HEADS = 32
GDN_KEY_DIM = GDN_K_HEADS * HEAD_DIM
GDN_VAL_DIM = GDN_V_HEADS * HEAD_DIM
GDN_CONV_DIM = 2 * GDN_KEY_DIM + GDN_VAL_DIM
GDN_MAIN_DIM = GDN_CONV_DIM + GDN_VAL_DIM
GDN_CONV_TAPS = 4

HGRN_HEADS = 16
HGRN_DIM = HGRN_HEADS * HEAD_DIM

VMEM_LIMIT_BYTES = 56 * 1024 * 1024
CARRY_ROWS = 8

GROUP = 2 * CHUNK
GDN_KH_PER_STEP = 4
GDN_ROWS = 256
HGRN_ROWS = 512
HGRN_BIG_HALVES = (32, 16, 8)
HGRN_SMALL_HALVES = (4, 2)
HGRN_DIAG = 2
CONV_SUB_ROWS = 256


def _params(*semantics):
    return pltpu.CompilerParams(dimension_semantics=semantics, vmem_limit_bytes=VMEM_LIMIT_BYTES)


def _mm(a, b):
    return jnp.dot(a.astype(BF16), b.astype(BF16), preferred_element_type=F32)


def _mm_nt(a, b):
    return lax.dot_general(a.astype(BF16), b.astype(BF16), (((1,), (1,)), ((), ())),
                           preferred_element_type=F32)


def _mm_tn(a, b):
    return lax.dot_general(a.astype(BF16), b.astype(BF16), (((0,), (0,)), ((), ())),
                           preferred_element_type=F32)


def _sigmoid(x):
    return 1.0 / (1.0 + jnp.exp(-x))


def _silu(x):
    return x * _sigmoid(x)


def _rmsnorm_kernel(x_ref, w_ref, o_ref):
    x = x_ref[...]
    y = x * lax.rsqrt(jnp.mean(x * x, axis=-1, keepdims=True) + EPS)
    o_ref[...] = (y * w_ref[...]).astype(o_ref.dtype)


def _rmsnorm(x, w, out_dtype, rows=512):
    m, d = x.shape
    return pl.pallas_call(
        _rmsnorm_kernel,
        grid=(m // rows,),
        in_specs=[pl.BlockSpec((rows, d), lambda i: (i, 0)),
                  pl.BlockSpec((1, d), lambda i: (0, 0))],
        out_specs=pl.BlockSpec((rows, d), lambda i: (i, 0)),
        out_shape=jax.ShapeDtypeStruct((m, d), out_dtype),
        compiler_params=_params("arbitrary"),
        name="rmsnorm",
    )(x, w.reshape(1, d))


def _weight_spec(k, tn, layer, first_tile=0, transposed=False):
    if transposed:
        return pl.BlockSpec((None, tn, k), lambda j, i: (layer, first_tile + j, 0))
    return pl.BlockSpec((None, k, tn), lambda j, i: (layer, 0, first_tile + j))


def _load_weight(w_ref, wb_ref):
    w = w_ref[...]
    if w.shape != wb_ref.shape:
        w = w.T
    wb_ref[...] = w.astype(BF16)


def _proj_kernel(x_ref, w_ref, o_ref, wb_ref):
    @pl.when(pl.program_id(1) == 0)
    def _():
        _load_weight(w_ref, wb_ref)

    o_ref[...] = jnp.dot(x_ref[...], wb_ref[...], preferred_element_type=F32).astype(o_ref.dtype)


def _proj(x, w, layer, n_cols, tm, tn, first_col=0, transposed=False):
    m, k = x.shape
    return pl.pallas_call(
        _proj_kernel,
        grid=(n_cols // tn, m // tm),
        in_specs=[pl.BlockSpec((tm, k), lambda j, i: (i, 0)),
                  _weight_spec(k, tn, layer, first_col // tn, transposed)],
        out_specs=pl.BlockSpec((tm, tn), lambda j, i: (i, j)),
        out_shape=jax.ShapeDtypeStruct((m, n_cols), BF16),
        scratch_shapes=[pltpu.VMEM((k, tn), BF16)],
        compiler_params=_params("arbitrary", "arbitrary"),
        name="proj",
    )(x, w)


def _proj_conv_kernel(x_ref, w_ref, cw_ref, o_ref, wb_ref, acc_ref):
    @pl.when(pl.program_id(1) == 0)
    def _():
        _load_weight(w_ref, wb_ref)
        acc_ref[0:CARRY_ROWS, :] = jnp.zeros((CARRY_ROWS, acc_ref.shape[1]), F32)

    cw = cw_ref[...]
    taps = GDN_CONV_TAPS
    tm = x_ref.shape[0]
    sub = CONV_SUB_ROWS
    for s in range(tm // sub):
        lo = CARRY_ROWS + s * sub
        acc = jnp.dot(x_ref[s * sub:(s + 1) * sub, :], wb_ref[...], preferred_element_type=F32)
        acc_ref[lo:lo + sub, :] = acc
        y = cw[taps - 1:taps] * acc
        for j in range(1, taps):
            y = y + cw[taps - 1 - j:taps - j] * acc_ref[lo - j:lo - j + sub, :]
        o_ref[s * sub:(s + 1) * sub, :] = _silu(y).astype(o_ref.dtype)
    acc_ref[0:CARRY_ROWS, :] = acc_ref[tm:tm + CARRY_ROWS, :]


def _proj_conv(x, w, layer, conv_w, n_cols, tm, tn, transposed=False):
    m, k = x.shape
    return pl.pallas_call(
        _proj_conv_kernel,
        grid=(n_cols // tn, m // tm),
        in_specs=[pl.BlockSpec((tm, k), lambda j, i: (i, 0)),
                  _weight_spec(k, tn, layer, 0, transposed),
                  pl.BlockSpec((GDN_CONV_TAPS, tn), lambda j, i: (0, j))],
        out_specs=pl.BlockSpec((tm, tn), lambda j, i: (i, j)),
        out_shape=jax.ShapeDtypeStruct((m, n_cols), BF16),
        scratch_shapes=[pltpu.VMEM((k, tn), BF16), pltpu.VMEM((CARRY_ROWS + tm, tn), F32)],
        compiler_params=_params("arbitrary", "arbitrary"),
        name="proj_conv",
    )(x, w, conv_w)


def _proj_residual_kernel(x_ref, w_ref, r_ref, o_ref, wb_ref):
    @pl.when(pl.program_id(1) == 0)
    def _():
        wb_ref[...] = w_ref[...].astype(BF16)

    o_ref[...] = r_ref[...] + jnp.dot(x_ref[...], wb_ref[...], preferred_element_type=F32)


def _proj_residual(x, w, layer, res, tm, tn):
    m, k = x.shape
    n = w.shape[2]
    return pl.pallas_call(
        _proj_residual_kernel,
        grid=(n // tn, m // tm),
        in_specs=[pl.BlockSpec((tm, k), lambda j, i: (i, 0)), _weight_spec(k, tn, layer),
                  pl.BlockSpec((tm, tn), lambda j, i: (i, j))],
        out_specs=pl.BlockSpec((tm, tn), lambda j, i: (i, j)),
        out_shape=jax.ShapeDtypeStruct((m, n), F32),
        scratch_shapes=[pltpu.VMEM((k, tn), BF16)],
        compiler_params=_params("arbitrary", "arbitrary"),
        name="proj_residual",
    )(x, w, res)


def _swiglu_up_kernel(x_ref, wg_ref, wu_ref, o_ref, wgb_ref, wub_ref):
    @pl.when(pl.program_id(1) == 0)
    def _():
        wgb_ref[...] = wg_ref[...].astype(BF16)
        wub_ref[...] = wu_ref[...].astype(BF16)

    x = x_ref[...]
    g = jnp.dot(x, wgb_ref[...], preferred_element_type=F32)
    u = jnp.dot(x, wub_ref[...], preferred_element_type=F32)
    o_ref[...] = (_silu(g) * u).astype(o_ref.dtype)


def _swiglu_up(x, w_gate, w_up, layer, tm, tn):
    m, k = x.shape
    n = w_gate.shape[2]
    return pl.pallas_call(
        _swiglu_up_kernel,
        grid=(n // tn, m // tm),
        in_specs=[pl.BlockSpec((tm, k), lambda j, i: (i, 0)), _weight_spec(k, tn, layer),
                  _weight_spec(k, tn, layer)],
        out_specs=pl.BlockSpec((tm, tn), lambda j, i: (i, j)),
        out_shape=jax.ShapeDtypeStruct((m, n), BF16),
        scratch_shapes=[pltpu.VMEM((k, tn), BF16), pltpu.VMEM((k, tn), BF16)],
        compiler_params=_params("arbitrary", "arbitrary"),
        name="swiglu_up",
    )(x, w_gate, w_up)


def _gdn_gate_kernel(x_ref, w_ref, alog_ref, dtb_ref, g_ref, beta_ref):
    ba = _mm_nt(w_ref[...], x_ref[...])
    nh = GDN_V_HEADS
    beta_ref[...] = _sigmoid(ba[:nh])
    a = ba[nh:] + dtb_ref[...]
    softplus = jnp.maximum(a, 0.0) + jnp.log1p(jnp.exp(-jnp.abs(a)))
    g = (-LOG2E * jnp.exp(alog_ref[...])) * softplus
    pos =lax.broadcasted_iota(jnp.int32, g.shape, 1) % CHUNK
    shift = 1
    while shift < CHUNK:
        g = g + jnp.where(pos >= shift, pltpu.roll(g, shift, 1), 0.0)
        shift *= 2
    g_ref[...] = g


def _gdn_gates(x, w_ba_t, a_log, dt_bias, rows=512):
    m, k = x.shape
    nh = GDN_V_HEADS
    return pl.pallas_call(
        _gdn_gate_kernel,
        grid=(m // rows,),
        in_specs=[pl.BlockSpec((rows, k), lambda i: (i, 0)),
                  pl.BlockSpec((2 * nh, k), lambda i: (0, 0)),
                  pl.BlockSpec((nh, 1), lambda i: (0, 0)),
                  pl.BlockSpec((nh, 1), lambda i: (0, 0))],
        out_specs=[pl.BlockSpec((nh, rows), lambda i: (0, i)),
                   pl.BlockSpec((nh, rows), lambda i: (0, i))],
        out_shape=[jax.ShapeDtypeStruct((nh, m), F32), jax.ShapeDtypeStruct((nh, m), F32)],
        compiler_params=_params("arbitrary"),
        name="gdn_gates",
    )(x, w_ba_t, a_log.reshape(nh, 1), dt_bias.reshape(nh, 1))


def _l2norm(x):
    return x * lax.rsqrt(jnp.sum(x * x, axis=-1, keepdims=True) + 1e-6)


def _gated_rmsnorm(o, w, z):
    y = o * lax.rsqrt(jnp.mean(o * o, axis=-1, keepdims=True) + EPS)
    return y * w * _silu(z)


def _unit_lower_inverses(a_list, row, col):
    eye = (row == col).astype(F32)
    same16 = (row // 16) == (col // 16)
    same32 = (row // 32) == (col // 32)
    off32 = same32 & jnp.logical_not(same16)
    off64 = jnp.logical_not(same32)
    ad = [jnp.where(same16, a, 0.0) for a in a_list]
    n = row.shape[0]
    p = [_mm(x, x) for x in ad]
    t = [eye - x for x in ad]
    for _ in range(2):
        both = [_mm(jnp.concatenate([x, y], axis=0), y) for x, y in zip(t, p)]
        t = [x + b[:n] for x, b in zip(t, both)]
        p = [b[n:] for b in both]
    t = [x + _mm(x, y) for x, y in zip(t, p)]
    for mask in (off32, off64):
        y = [_mm(jnp.where(mask, a, 0.0), x) for a, x in zip(a_list, t)]
        t = [x - _mm(x, yy) for x, yy in zip(t, y)]
    return t


def _gdn_kernel(q_ref, k_ref, v_ref, z_ref, grow_ref, gcol_ref, bcol_ref, hn_ref, o_ref, s_ref):
    hb = pl.program_id(0)
    rows = q_ref.shape[0]
    n_groups = rows // GROUP
    n_chunks = rows // CHUNK
    per_group = GROUP // CHUNK

    @pl.when(pl.program_id(1) == 0)
    def _():
        s_ref[...] = jnp.zeros_like(s_ref)

    q_all = q_ref[...].astype(F32)
    k_all = k_ref[...].astype(F32)
    v_all = v_ref[...].astype(F32)
    hn = hn_ref[...]

    def lanes(x, i):
        return x[:, i * HEAD_DIM:(i + 1) * HEAD_DIM]

    qs = [_l2norm(lanes(q_all, j)) * (HEAD_DIM ** -0.5) for j in range(GDN_KH_PER_STEP)]
    ks = [_l2norm(lanes(k_all, j)) for j in range(GDN_KH_PER_STEP)]

    row = lax.broadcasted_iota(jnp.int32, (GROUP, GROUP), 0)
    col = lax.broadcasted_iota(jnp.int32, (GROUP, GROUP), 1)
    same_chunk = (row // CHUNK) == (col // CHUNK)
    causal = (row >= col) & same_chunk
    strict = row > col

    heads = [(j, e) for j in range(GDN_KH_PER_STEP) for e in range(2)]
    head_lane = lax.broadcasted_iota(jnp.int32, gcol_ref.shape, 1)
    gcol_all = gcol_ref[...]
    bcol_all = bcol_ref[...]
    g_col, b_col, g_row = [], [], []
    for j, e in heads:
        head = 2 * (GDN_KH_PER_STEP * hb + j) + e
        sel = head_lane == head
        g_col.append(jnp.sum(jnp.where(sel, gcol_all, 0.0), axis=1, keepdims=True))
        b_col.append(jnp.sum(jnp.where(sel, bcol_all, 0.0), axis=1, keepdims=True))
        g_row.append(grow_ref[pl.ds(head, 1), :])

    def grp(x, g):
        return x[g * GROUP:(g + 1) * GROUP]

    kq = [[_mm_nt(jnp.concatenate([grp(qs[j], g), grp(ks[j], g)], axis=0), grp(ks[j], g))
           for g in range(n_groups)] for j in range(GDN_KH_PER_STEP)]

    items = [(h, g) for h in range(len(heads)) for g in range(n_groups)]
    a_list, aqk_list, rhs_list, qd_list, kd_list = [], [], [], [], []
    for h, g in items:
        j, e = heads[h]
        gc = grp(g_col[h], g)
        bc = grp(b_col[h], g)
        gr = g_row[h][:, g * GROUP:(g + 1) * GROUP]
        decay = jnp.exp2(jnp.where(causal, gc - gr, -jnp.inf))
        a_list.append(jnp.where(strict, bc * kq[j][g][GROUP:] * decay, 0.0))
        aqk_list.append(kq[j][g][:GROUP] * decay)
        eg = jnp.exp2(gc)
        kg = grp(ks[j], g)
        vg = grp(lanes(v_all, 2 * j + e), g)
        rhs_list.append(jnp.concatenate([vg * bc, kg * (bc * eg)], axis=1))
        g_last = jnp.concatenate(
            [jnp.broadcast_to(gc[(i + 1) * CHUNK - 1:(i + 1) * CHUNK], (CHUNK, 1)) for i in range(per_group)],
            axis=0)
        qd_list.append(grp(qs[j], g) * eg)
        kd_list.append(kg * jnp.exp2(g_last - gc))

    t_list = _unit_lower_inverses(a_list, row, col)
    uw_list = [_mm(t, r) for t, r in zip(t_list, rhs_list)]

    states = [s_ref[h] for h in range(len(heads))]
    for c in range(n_chunks):
        g, i = divmod(c, per_group)
        sl = slice(i * CHUNK, (i + 1) * CHUNK)
        idx = [h * n_groups + g for h in range(len(heads))]
        ws_qs = [_mm(jnp.concatenate([uw_list[n][sl, HEAD_DIM:], qd_list[n][sl]], axis=0), states[h])
                 for h, n in enumerate(idx)]
        v_new = [uw_list[n][sl, :HEAD_DIM] - r[:CHUNK] for n, r in zip(idx, ws_qs)]
        intra = [_mm(aqk_list[n][sl, i * CHUNK:(i + 1) * CHUNK], vn) for n, vn in zip(idx, v_new)]
        upd = [_mm_tn(kd_list[n][sl], vn) for n, vn in zip(idx, v_new)]
        for h in range(len(heads)):
            j, e = heads[h]
            g_last = g_col[h][(c + 1) * CHUNK - 1:(c + 1) * CHUNK]
            states[h] = states[h] * jnp.exp2(g_last) + upd[h]
            o = ws_qs[h][CHUNK:] + intra[h]
            z = z_ref[c * CHUNK:(c + 1) * CHUNK, (2 * j + e) * HEAD_DIM:(2 * j + e + 1) * HEAD_DIM].astype(F32)
            o_ref[c * CHUNK:(c + 1) * CHUNK, (2 * j + e) * HEAD_DIM:(2 * j + e + 1) * HEAD_DIM] = (
                _gated_rmsnorm(o, hn, z).astype(o_ref.dtype))
    for h in range(len(heads)):
        s_ref[h] = states[h]


def _gdn_mixer(qkv, z, g_row, g_col, b_col, head_norm):
    m = qkv.shape[0]
    rows = GDN_ROWS
    nh = GDN_V_HEADS
    kw = GDN_KH_PER_STEP * HEAD_DIM
    vw = 2 * kw
    kb = GDN_KEY_DIM // kw
    vb = 2 * GDN_KEY_DIM // vw
    return pl.pallas_call(
        _gdn_kernel,
        grid=(GDN_K_HEADS // GDN_KH_PER_STEP, m // rows),
        in_specs=[
            pl.BlockSpec((rows, kw), lambda h, t: (t, h)),
            pl.BlockSpec((rows, kw), lambda h, t: (t, kb + h)),
            pl.BlockSpec((rows, vw), lambda h, t: (t, vb + h)),
            pl.BlockSpec((rows, vw), lambda h, t: (t, h)),
            pl.BlockSpec((nh, rows), lambda h, t: (0, t)),
            pl.BlockSpec((rows, nh), lambda h, t: (t, 0)),
            pl.BlockSpec((rows, nh), lambda h, t: (t, 0)),
            pl.BlockSpec((1, HEAD_DIM), lambda h, t: (0, 0)),
        ],
        out_specs=pl.BlockSpec((rows, vw), lambda h, t: (t, h)),
        out_shape=jax.ShapeDtypeStruct((m, GDN_VAL_DIM), BF16),
        scratch_shapes=[pltpu.VMEM((2 * GDN_KH_PER_STEP, HEAD_DIM, HEAD_DIM), F32)],
        compiler_params=_params("arbitrary", "arbitrary"),
        name="gdn_mixer",
    )(qkv, qkv, qkv, z, g_row, g_col, b_col, head_norm.reshape(1, HEAD_DIM))


def _row_broadcast(x, first, period, rows):
    parts = [jnp.broadcast_to(x[r:r + 1], (period, x.shape[1])) for r in range(first, rows, period)]
    return jnp.concatenate(parts, axis=0)


def _hgrn_kernel(q_ref, f_ref, i_ref, g_ref, lb_ref, hn_ref, o_ref, s_ref):
    rows = q_ref.shape[0]
    n_groups = rows // GROUP
    n_chunks = rows // CHUNK
    per_group = GROUP // CHUNK

    @pl.when(pl.program_id(1) == 0)
    def _():
        s_ref[...] = jnp.zeros_like(s_ref)

    lbs = lb_ref[...]
    mx = jnp.max(lbs, axis=0, keepdims=True)
    ex = jnp.exp(lbs - mx)
    sm = ex / jnp.sum(ex, axis=0, keepdims=True)
    lb = (sm[0:1] + sm[1:2]) - sm[0:1]

    q = _silu(q_ref[...].astype(F32))
    fl = f_ref[...].astype(F32)
    v = i_ref[...].astype(F32)
    en = jnp.exp(-jnp.abs(fl))
    rc = 1.0 / (1.0 + en)
    pos = fl >= 0.0
    sig = jnp.where(pos, rc, en * rc)
    nsig = jnp.where(pos, en * rc, rc)
    logf = jnp.log(lb + (1.0 - lb) * sig)
    kk = (1.0 - lb) * nsig

    rowi = lax.broadcasted_iota(jnp.int32, (rows, HEAD_DIM), 0)
    row = lax.broadcasted_iota(jnp.int32, (GROUP, GROUP), 0)
    col = lax.broadcasted_iota(jnp.int32, (GROUP, GROUP), 1)

    tri = (((row // CHUNK) == (col // CHUNK)) & (row >= col)).astype(BF16)
    logf2 = logf * LOG2E
    hi = logf2.astype(BF16)
    rest = logf2 - hi.astype(F32)
    mid = rest.astype(BF16)
    lo = (rest - mid.astype(F32)).astype(BF16)
    pieces = jnp.concatenate([hi, mid, lo], axis=1)

    def mid_range(s):
        mid_row = (row // (2 * s)) * (2 * s) + (s - 1)
        return ((col > mid_row) & (col <= row)) | ((col > row) & (col <= mid_row))

    small_ranges = jnp.concatenate([mid_range(s).astype(BF16) for s in HGRN_SMALL_HALVES], axis=0)
    b_parts, small_parts = [], []
    for g in range(n_groups):
        pg = pieces[g * GROUP:(g + 1) * GROUP]
        s3 = jnp.dot(tri, pg, preferred_element_type=F32)
        b_parts.append(s3[:, :HEAD_DIM] + s3[:, HEAD_DIM:2 * HEAD_DIM] + s3[:, 2 * HEAD_DIM:])
        s2 = jnp.dot(small_ranges, pg[:, :2 * HEAD_DIM], preferred_element_type=F32)
        small_parts.append(s2[:, :HEAD_DIM] + s2[:, HEAD_DIM:])
    b = jnp.concatenate(b_parts, axis=0)
    b_last = _row_broadcast(b, CHUNK - 1, CHUNK, rows)
    qd = q * jnp.exp2(b)
    kd = kk * jnp.exp2(b_last - b)

    halves = HGRN_BIG_HALVES + HGRN_SMALL_HALVES
    level_x = []
    for s in HGRN_BIG_HALVES:
        b_mid = _row_broadcast(b, s - 1, 2 * s, rows)
        lower = (rowi % (2 * s)) >= s
        level_x.append(jnp.where(lower, q, kk) * jnp.exp2(-jnp.abs(b - b_mid)))
    for n, s in enumerate(HGRN_SMALL_HALVES):
        expo = jnp.concatenate([p[n * GROUP:(n + 1) * GROUP] for p in small_parts], axis=0)
        lower = (rowi % (2 * s)) >= s
        level_x.append(jnp.where(lower, q, kk) * jnp.exp2(expo))

    def rot8(x, d):
        return pltpu.roll(x.reshape(rows // 8, 8, HEAD_DIM), d, 1).reshape(rows, HEAD_DIM)

    diags = [jnp.sum(q * kk, axis=1, keepdims=True)]
    for d in range(1, HGRN_DIAG):
        e = jnp.exp2(jnp.minimum(b - rot8(b, d), 0.0))
        diags.append(jnp.sum(q * rot8(kk, d) * e, axis=1, keepdims=True))

    def grp(x, g):
        return x[g * GROUP:(g + 1) * GROUP]

    def chk(x, c):
        return x[c * CHUNK:(c + 1) * CHUNK]

    level_mm = [[_mm_nt(grp(x, g), grp(x, g)) for x in level_x] for g in range(n_groups)]
    kv = [_mm_tn(chk(v, c), chk(kd, c)) for c in range(n_chunks)]

    scores = []
    for g in range(n_groups):
        sc = jnp.zeros((GROUP, GROUP), F32)
        for s, mm in zip(halves, level_mm[g]):
            pair = ((row // (2 * s)) == (col // (2 * s))) & ((row % (2 * s)) >= s) & ((col % (2 * s)) < s)
            sc = jnp.where(pair, mm, sc)
        same_diag = (row // HGRN_DIAG) == (col // HGRN_DIAG)
        for d in range(HGRN_DIAG):
            sc = jnp.where(same_diag & (row - col == d), grp(diags[d], g), sc)
        scores.append(sc)

    state_t = s_ref[...]
    states = []
    for c in range(n_chunks):
        states.append(state_t)
        f_last = jnp.exp2(b[(c + 1) * CHUNK - 1:(c + 1) * CHUNK])
        state_t = state_t * f_last + kv[c]
    s_ref[...] = state_t

    intra = [_mm(scores[g], grp(v, g)) for g in range(n_groups)]
    inter = [_mm_nt(chk(qd, c), states[c]) for c in range(n_chunks)]
    hn = hn_ref[...]
    for c in range(n_chunks):
        g, i = divmod(c, per_group)
        o = inter[c] + intra[g][i * CHUNK:(i + 1) * CHUNK]
        gate = g_ref[c * CHUNK:(c + 1) * CHUNK, :].astype(F32)
        o_ref[c * CHUNK:(c + 1) * CHUNK, :] = _gated_rmsnorm(o, hn, gate).astype(o_ref.dtype)


def _hgrn_mixer(proj, lower_bounds, head_norm):
    m = proj.shape[0]
    rows = HGRN_ROWS
    nb = HGRN_DIM // HEAD_DIM
    return pl.pallas_call(
        _hgrn_kernel,
        grid=(HGRN_HEADS, m // rows),
        in_specs=[
            pl.BlockSpec((rows, HEAD_DIM), lambda h, t: (t, h)),
            pl.BlockSpec((rows, HEAD_DIM), lambda h, t: (t, nb + h)),
            pl.BlockSpec((rows, HEAD_DIM), lambda h, t: (t, 2 * nb + h)),
            pl.BlockSpec((rows, HEAD_DIM), lambda h, t: (t, 3 * nb + h)),
            pl.BlockSpec((2, HEAD_DIM), lambda h, t: (0, h)),
            pl.BlockSpec((1, HEAD_DIM), lambda h, t: (0, 0)),
        ],
        out_specs=pl.BlockSpec((rows, HEAD_DIM), lambda h, t: (t, h)),
        out_shape=jax.ShapeDtypeStruct((m, HGRN_DIM), BF16),
        scratch_shapes=[pltpu.VMEM((HEAD_DIM, HEAD_DIM), F32)],
        compiler_params=_params("arbitrary", "arbitrary"),
        name="hgrn_mixer",
    )(proj, proj, proj, proj, lower_bounds, head_norm.reshape(1, HEAD_DIM))


def _ffn(h, layer, norm_w, w_gate, w_up, w_down):
    hn = _rmsnorm(h, norm_w[layer], BF16)
    act = _swiglu_up(hn, w_gate, w_up, layer, tm=1024, tn=512)
    return _proj_residual(act, w_down, layer, h, tm=512, tn=512)


def kernel(x, gdn_norm, gdn_w_in, gdn_conv, gdn_a_log, gdn_dt_bias, gdn_head_norm, gdn_w_out,
           hgrn_norm, hgrn_w_in, hgrn_lower_bounds, hgrn_head_norm, hgrn_w_out,
           ffn_norm, ffn_w_gate, ffn_w_up, ffn_w_down, final_norm):
    assert x.shape == (1, SEQ, D_MODEL)
    h = x.reshape(SEQ, D_MODEL)

    hn = _rmsnorm(h, gdn_norm[0], BF16)
    w_in_t = jnp.swapaxes(gdn_w_in, 1, 2)
    qkv = _proj_conv(hn, w_in_t, 0, gdn_conv[0], GDN_CONV_DIM, tm=1024, tn=1024, transposed=True)
    z = _proj(hn, w_in_t, 0, GDN_VAL_DIM, tm=1024, tn=1024, first_col=GDN_CONV_DIM, transposed=True)
    w_ba_t = w_in_t[0, GDN_MAIN_DIM:]
    g_row, b_row = _gdn_gates(hn, w_ba_t, gdn_a_log[0], gdn_dt_bias[0])
    mixed = _gdn_mixer(qkv, z, g_row, g_row.T, b_row.T, gdn_head_norm[0])
    h = _proj_residual(mixed, gdn_w_out, 0, h, tm=1024, tn=512)
    h = _ffn(h, 0, ffn_norm, ffn_w_gate, ffn_w_up, ffn_w_down)

    hn = _rmsnorm(h, hgrn_norm[0], BF16)
    proj = _proj(hn, hgrn_w_in, 0, 4 * HGRN_DIM, tm=1024, tn=1024)
    mixed = _hgrn_mixer(proj, hgrn_lower_bounds, hgrn_head_norm[0])
    h = _proj_residual(mixed, hgrn_w_out, 0, h, tm=1024, tn=512)
    h = _ffn(h, 1, ffn_norm, ffn_w_gate, ffn_w_up, ffn_w_down)

    return _rmsnorm(h, final_norm, F32).reshape(1, SEQ, D_MODEL)
```

```python
import jax
import jax.numpy as jnp
from jax import lax
from jax.experimental import pallas as pl
from jax.experimental.pallas import tpu as pltpu

F32 = jnp.float32
BF16 = jnp.bfloat16

D_MODEL = 2048
SEQ = 8192
CHUNK = 64
HEAD_DIM = 128
EPS = 1e-6
LOG2E = 1.4426950408889634

GDN_K_HEADS = 16
GDN_V_HEADS = 32
GDN_KEY_DIM = GDN_K_HEADS * HEAD_DIM
GDN_VAL_DIM = GDN_V_HEADS * HEAD_DIM
GDN_CONV_DIM = 2 * GDN_KEY_DIM + GDN_VAL_DIM
GDN_MAIN_DIM = GDN_CONV_DIM + GDN_VAL_DIM
GDN_CONV_TAPS = 4

HGRN_HEADS = 16
HGRN_DIM = HGRN_HEADS * HEAD_DIM

VMEM_LIMIT_BYTES = 56 * 1024 * 1024
CARRY_ROWS = 8

GROUP = 2 * CHUNK
GDN_KH_PER_STEP = 4
GDN_ROWS = 512
HGRN_ROWS = 2048
HGRN_BIG_HALVES = (32, 16, 8)
HGRN_SMALL_HALVES = (4, 2)
HGRN_DIAG = 2


def _params(*semantics):
    return pltpu.CompilerParams(dimension_semantics=semantics, vmem_limit_bytes=VMEM_LIMIT_BYTES)


def _mm(a, b):
    return jnp.dot(a.astype(BF16), b.astype(BF16), preferred_element_type=F32)


def _mm_nt(a, b):
    return lax.dot_general(a.astype(BF16), b.astype(BF16), (((1,), (1,)), ((), ())),
                           preferred_element_type=F32)


def _mm_tn(a, b):
    return lax.dot_general(a.astype(BF16), b.astype(BF16), (((0,), (0,)), ((), ())),
                           preferred_element_type=F32)


def _sigmoid(x):
    return 1.0 / (1.0 + jnp.exp(-x))


def _silu(x):
    return x * _sigmoid(x)


def _rmsnorm_kernel(x_ref, w_ref, o_ref):
    x = x_ref[...]
    y = x * lax.rsqrt(jnp.mean(x * x, axis=-1, keepdims=True) + EPS)
    o_ref[...] = (y * w_ref[...]).astype(o_ref.dtype)


def _rmsnorm(x, w, out_dtype, rows=512):
    m, d = x.shape
    return pl.pallas_call(
        _rmsnorm_kernel,
        grid=(m // rows,),
        in_specs=[pl.BlockSpec((rows, d), lambda i: (i, 0)),
                  pl.BlockSpec((1, d), lambda i: (0, 0))],
        out_specs=pl.BlockSpec((rows, d), lambda i: (i, 0)),
        out_shape=jax.ShapeDtypeStruct((m, d), out_dtype),
        compiler_params=_params("arbitrary"),
        name="rmsnorm",
    )(x, w.reshape(1, d))


def _weight_spec(k, tn, layer, first_tile=0, transposed=False):
    if transposed:
        return pl.BlockSpec((None, tn, k), lambda j, i: (layer, first_tile + j, 0))
    return pl.BlockSpec((None, k, tn), lambda j, i: (layer, 0, first_tile + j))


def _load_weight(w_ref, wb_ref):
    w = w_ref[...]
    if w.shape != wb_ref.shape:
        w = w.T
    wb_ref[...] = w.astype(BF16)


def _proj_kernel(x_ref, w_ref, o_ref, wb_ref):
    @pl.when(pl.program_id(1) == 0)
    def _():
        _load_weight(w_ref, wb_ref)

    o_ref[...] = jnp.dot(x_ref[...], wb_ref[...], preferred_element_type=F32).astype(o_ref.dtype)


def _proj(x, w, layer, n_cols, tm, tn, first_col=0, transposed=False):
    m, k = x.shape
    return pl.pallas_call(
        _proj_kernel,
        grid=(n_cols // tn, m // tm),
        in_specs=[pl.BlockSpec((tm, k), lambda j, i: (i, 0)),
                  _weight_spec(k, tn, layer, first_col // tn, transposed)],
        out_specs=pl.BlockSpec((tm, tn), lambda j, i: (i, j)),
        out_shape=jax.ShapeDtypeStruct((m, n_cols), BF16),
        scratch_shapes=[pltpu.VMEM((k, tn), BF16)],
        compiler_params=_params("arbitrary", "arbitrary"),
        name="proj",
    )(x, w)


def _proj_residual_kernel(x_ref, w_ref, r_ref, o_ref, wb_ref):
    @pl.when(pl.program_id(1) == 0)
    def _():
        wb_ref[...] = w_ref[...].astype(BF16)

    o_ref[...] = r_ref[...] + jnp.dot(x_ref[...], wb_ref[...], preferred_element_type=F32)


def _proj_residual(x, w, layer, res, tm, tn):
    m, k = x.shape
    n = w.shape[2]
    return pl.pallas_call(
        _proj_residual_kernel,
        grid=(n // tn, m // tm),
        in_specs=[pl.BlockSpec((tm, k), lambda j, i: (i, 0)), _weight_spec(k, tn, layer),
                  pl.BlockSpec((tm, tn), lambda j, i: (i, j))],
        out_specs=pl.BlockSpec((tm, tn), lambda j, i: (i, j)),
        out_shape=jax.ShapeDtypeStruct((m, n), F32),
        scratch_shapes=[pltpu.VMEM((k, tn), BF16)],
        compiler_params=_params("arbitrary", "arbitrary"),
        name="proj_residual",
    )(x, w, res)


def _proj_residual_norm_kernel(x_ref, w_ref, r_ref, nw_ref, o_ref, on_ref, wb_ref):
    @pl.when(pl.program_id(0) == 0)
    def _():
        wb_ref[...] = w_ref[...].astype(BF16)

    h = r_ref[...] + jnp.dot(x_ref[...], wb_ref[...], preferred_element_type=F32)
    o_ref[...] = h
    y = h * lax.rsqrt(jnp.mean(h * h, axis=-1, keepdims=True) + EPS)
    on_ref[...] = (y * nw_ref[...]).astype(on_ref.dtype)


def _proj_residual_norm(x, w, layer, res, norm_w, tm):
    m, k = x.shape
    n = w.shape[2]
    return pl.pallas_call(
        _proj_residual_norm_kernel,
        grid=(m // tm,),
        in_specs=[pl.BlockSpec((tm, k), lambda i: (i, 0)),
                  pl.BlockSpec((None, k, n), lambda i: (layer, 0, 0), pipeline_mode=pl.Buffered(1)),
                  pl.BlockSpec((tm, n), lambda i: (i, 0)),
                  pl.BlockSpec((1, n), lambda i: (0, 0))],
        out_specs=[pl.BlockSpec((tm, n), lambda i: (i, 0)), pl.BlockSpec((tm, n), lambda i: (i, 0))],
        out_shape=[jax.ShapeDtypeStruct((m, n), F32), jax.ShapeDtypeStruct((m, n), BF16)],
        scratch_shapes=[pltpu.VMEM((k, n), BF16)],
        compiler_params=_params("arbitrary"),
        name="proj_residual_norm",
    )(x, w, res, norm_w.reshape(1, n))


def _swiglu_up_kernel(x_ref, wg_ref, wu_ref, o_ref, wgb_ref, wub_ref):
    @pl.when(pl.program_id(1) == 0)
    def _():
        wgb_ref[...] = wg_ref[...].astype(BF16)
        wub_ref[...] = wu_ref[...].astype(BF16)

    x = x_ref[...]
    g = jnp.dot(x, wgb_ref[...], preferred_element_type=F32)
    u = jnp.dot(x, wub_ref[...], preferred_element_type=F32)
    o_ref[...] = (_silu(g) * u).astype(o_ref.dtype)


def _swiglu_up(x, w_gate, w_up, layer, tm, tn):
    m, k = x.shape
    n = w_gate.shape[2]
    return pl.pallas_call(
        _swiglu_up_kernel,
        grid=(n // tn, m // tm),
        in_specs=[pl.BlockSpec((tm, k), lambda j, i: (i, 0)), _weight_spec(k, tn, layer),
                  _weight_spec(k, tn, layer)],
        out_specs=pl.BlockSpec((tm, tn), lambda j, i: (i, j)),
        out_shape=jax.ShapeDtypeStruct((m, n), BF16),
        scratch_shapes=[pltpu.VMEM((k, tn), BF16), pltpu.VMEM((k, tn), BF16)],
        compiler_params=_params("arbitrary", "arbitrary"),
        name="swiglu_up",
    )(x, w_gate, w_up)


def _gdn_gate_kernel(x_ref, w_ref, alog_ref, dtb_ref, g_ref, beta_ref):
    ba = _mm_nt(w_ref[...], x_ref[...])
    nh = GDN_V_HEADS
    beta_ref[...] = _sigmoid(ba[:nh])
    a = ba[nh:] + dtb_ref[...]
    softplus = jnp.maximum(a, 0.0) + jnp.log1p(jnp.exp(-jnp.abs(a)))
    g = (-LOG2E * jnp.exp(alog_ref[...])) * softplus
    pos = lax.broadcasted_iota(jnp.int32, g.shape, 1) % CHUNK
    shift = 1
    while shift < CHUNK:
        g = g + jnp.where(pos >= shift, pltpu.roll(g, shift, 1), 0.0)
        shift *= 2
    g_ref[...] = g


def _gdn_gates(x, w_ba_t, a_log, dt_bias, rows=512):
    m, k = x.shape
    nh = GDN_V_HEADS
    return pl.pallas_call(
        _gdn_gate_kernel,
        grid=(m // rows,),
        in_specs=[pl.BlockSpec((rows, k), lambda i: (i, 0)),
                  pl.BlockSpec((2 * nh, k), lambda i: (0, 0)),
                  pl.BlockSpec((nh, 1), lambda i: (0, 0)),
                  pl.BlockSpec((nh, 1), lambda i: (0, 0))],
        out_specs=[pl.BlockSpec((nh, rows), lambda i: (0, i)),
                   pl.BlockSpec((nh, rows), lambda i: (0, i))],
        out_shape=[jax.ShapeDtypeStruct((nh, m), F32), jax.ShapeDtypeStruct((nh, m), F32)],
        compiler_params=_params("arbitrary"),
        name="gdn_gates",
    )(x, w_ba_t, a_log.reshape(nh, 1), dt_bias.reshape(nh, 1))


def _conv_silu(x_ref, carry_ref, w_ref):
    x = x_ref[...].astype(F32)
    rows, width = x.shape
    xe = jnp.concatenate([carry_ref[...], x], axis=0).reshape(rows // 8 + 1, 8, width)
    in_tile = lax.broadcasted_iota(jnp.int32, (1, 8, width), 1)
    w = w_ref[...]
    taps = GDN_CONV_TAPS
    y = w[taps - 1:taps] * x
    for j in range(1, taps):
        r = pltpu.roll(xe, j, 1)
        shifted = jnp.where(in_tile < j, r[:-1], r[1:]).reshape(rows, width)
        y = y + w[taps - 1 - j:taps - j] * shifted
    carry_ref[...] = x[rows - CARRY_ROWS:]
    return _silu(y)


def _l2norm(x):
    return x * lax.rsqrt(jnp.sum(x * x, axis=-1, keepdims=True) + 1e-6)


def _gated_rmsnorm(o, w, z):
    y = o * lax.rsqrt(jnp.mean(o * o, axis=-1, keepdims=True) + EPS)
    return y * w * _silu(z)


def _unit_lower_inverses(a_list, row, col):
    eye = (row == col).astype(F32)
    same16 = (row // 16) == (col // 16)
    same32 = (row // 32) == (col // 32)
    off32 = same32 & jnp.logical_not(same16)
    off64 = jnp.logical_not(same32)
    ad = [jnp.where(same16, a, 0.0) for a in a_list]
    n = row.shape[0]
    p = [_mm(x, x) for x in ad]
    t = [eye - x for x in ad]
    for _ in range(2):
        both = [_mm(jnp.concatenate([x, y], axis=0), y) for x, y in zip(t, p)]
        t = [x + b[:n] for x, b in zip(t, both)]
        p = [b[n:] for b in both]
    t = [x + _mm(x, y) for x, y in zip(t, p)]
    for mask in (off32, off64):
        y = [_mm(jnp.where(mask, a, 0.0), x) for a, x in zip(a_list, t)]
        t = [x - _mm(x, yy) for x, yy in zip(t, y)]
    return t


def _gdn_kernel(q_ref, k_ref, v_ref, z_ref, wq_ref, wk_ref, wv_ref, grow_ref,
                gcol_ref, bcol_ref, hn_ref, o_ref, s_ref, cq_ref, ck_ref, cv_ref):
    hb = pl.program_id(0)
    rows = q_ref.shape[0]
    n_groups = rows // GROUP
    n_chunks = rows // CHUNK
    per_group = GROUP // CHUNK

    @pl.when(pl.program_id(1) == 0)
    def _():
        s_ref[...] = jnp.zeros_like(s_ref)
        cq_ref[...] = jnp.zeros_like(cq_ref)
        ck_ref[...] = jnp.zeros_like(ck_ref)
        cv_ref[...] = jnp.zeros_like(cv_ref)

    q_all = _conv_silu(q_ref, cq_ref, wq_ref)
    k_all = _conv_silu(k_ref, ck_ref, wk_ref)
    v_all = _conv_silu(v_ref, cv_ref, wv_ref)
    hn = hn_ref[...]

    def lanes(x, i):
        return x[:, i * HEAD_DIM:(i + 1) * HEAD_DIM]

    qs = [_l2norm(lanes(q_all, j)) * (HEAD_DIM ** -0.5) for j in range(GDN_KH_PER_STEP)]
    ks = [_l2norm(lanes(k_all, j)) for j in range(GDN_KH_PER_STEP)]

    row = lax.broadcasted_iota(jnp.int32, (GROUP, GROUP), 0)
    col = lax.broadcasted_iota(jnp.int32, (GROUP, GROUP), 1)
    same_chunk = (row // CHUNK) == (col // CHUNK)
    causal = (row >= col) & same_chunk
    strict = row > col

    heads = [(j, e) for j in range(GDN_KH_PER_STEP) for e in range(2)]
    head_lane = lax.broadcasted_iota(jnp.int32, gcol_ref.shape, 1)
    gcol_all = gcol_ref[...]
    bcol_all = bcol_ref[...]
    g_col, b_col, g_row = [], [], []
    for j, e in heads:
        head = 2 * (GDN_KH_PER_STEP * hb + j) + e
        sel = head_lane == head
        g_col.append(jnp.sum(jnp.where(sel, gcol_all, 0.0), axis=1, keepdims=True))
        b_col.append(jnp.sum(jnp.where(sel, bcol_all, 0.0), axis=1, keepdims=True))
        g_row.append(grow_ref[pl.ds(head, 1), :])

    def grp(x, g):
        return x[g * GROUP:(g + 1) * GROUP]

    kq = [[_mm_nt(jnp.concatenate([grp(qs[j], g), grp(ks[j], g)], axis=0), grp(ks[j], g))
           for g in range(n_groups)] for j in range(GDN_KH_PER_STEP)]

    items = [(h, g) for h in range(len(heads)) for g in range(n_groups)]
    a_list, aqk_list, rhs_list, qd_list, kd_list = [], [], [], [], []
    for h, g in items:
        j, e = heads[h]
        gc = grp(g_col[h], g)
        bc = grp(b_col[h], g)
        gr = g_row[h][:, g * GROUP:(g + 1) * GROUP]
        decay = jnp.exp2(jnp.where(causal, gc - gr, -jnp.inf))
        a_list.append(jnp.where(strict, bc * kq[j][g][GROUP:] * decay, 0.0))
        aqk_list.append(kq[j][g][:GROUP] * decay)
        eg = jnp.exp2(gc)
        kg = grp(ks[j], g)
        vg = grp(lanes(v_all, 2 * j + e), g)
        rhs_list.append(jnp.concatenate([vg * bc, kg * (bc * eg)], axis=1))
        g_last = jnp.concatenate(
            [jnp.broadcast_to(gc[(i + 1) * CHUNK - 1:(i + 1) * CHUNK], (CHUNK, 1)) for i in range(per_group)],
            axis=0)
        qd_list.append(grp(qs[j], g) * eg)
        kd_list.append(kg * jnp.exp2(g_last - gc))

    t_list = _unit_lower_inverses(a_list, row, col)
    uw_list = [_mm(t, r) for t, r in zip(t_list, rhs_list)]

    states = [s_ref[h] for h in range(len(heads))]
    for c in range(n_chunks):
        g, i = divmod(c, per_group)
        sl = slice(i * CHUNK, (i + 1) * CHUNK)
        idx = [h * n_groups + g for h in range(len(heads))]
        ws_qs = [_mm(jnp.concatenate([uw_list[n][sl, HEAD_DIM:], qd_list[n][sl]], axis=0), states[h])
                 for h, n in enumerate(idx)]
        v_new = [uw_list[n][sl, :HEAD_DIM] - r[:CHUNK] for n, r in zip(idx, ws_qs)]
        intra = [_mm(aqk_list[n][sl, i * CHUNK:(i + 1) * CHUNK], vn) for n, vn in zip(idx, v_new)]
        upd = [_mm_tn(kd_list[n][sl], vn) for n, vn in zip(idx, v_new)]
        for h in range(len(heads)):
            j, e = heads[h]
            g_last = g_col[h][(c + 1) * CHUNK - 1:(c + 1) * CHUNK]
            states[h] = states[h] * jnp.exp2(g_last) + upd[h]
            o = ws_qs[h][CHUNK:] + intra[h]
            z = z_ref[c * CHUNK:(c + 1) * CHUNK, (2 * j + e) * HEAD_DIM:(2 * j + e + 1) * HEAD_DIM].astype(F32)
            o_ref[c * CHUNK:(c + 1) * CHUNK, (2 * j + e) * HEAD_DIM:(2 * j + e + 1) * HEAD_DIM] = (
                _gated_rmsnorm(o, hn, z).astype(o_ref.dtype))
    for h in range(len(heads)):
        s_ref[h] = states[h]


def _gdn_mixer(proj, conv_w, g_row, g_col, b_col, head_norm):
    m = proj.shape[0]
    rows = GDN_ROWS
    nh = GDN_V_HEADS
    kw = GDN_KH_PER_STEP * HEAD_DIM
    vw = 2 * kw
    kb = GDN_KEY_DIM // kw
    vb = 2 * GDN_KEY_DIM // vw
    zb = GDN_CONV_DIM // vw
    return pl.pallas_call(
        _gdn_kernel,
        grid=(GDN_K_HEADS // GDN_KH_PER_STEP, m // rows),
        in_specs=[
            pl.BlockSpec((rows, kw), lambda h, t: (t, h)),
            pl.BlockSpec((rows, kw), lambda h, t: (t, kb + h)),
            pl.BlockSpec((rows, vw), lambda h, t: (t, vb + h)),
            pl.BlockSpec((rows, vw), lambda h, t: (t, zb + h)),
            pl.BlockSpec((GDN_CONV_TAPS, kw), lambda h, t: (0, h)),
            pl.BlockSpec((GDN_CONV_TAPS, kw), lambda h, t: (0, kb + h)),
            pl.BlockSpec((GDN_CONV_TAPS, vw), lambda h, t: (0, vb + h)),
            pl.BlockSpec((nh, rows), lambda h, t: (0, t)),
            pl.BlockSpec((rows, nh), lambda h, t: (t, 0)),
            pl.BlockSpec((rows, nh), lambda h, t: (t, 0)),
            pl.BlockSpec((1, HEAD_DIM), lambda h, t: (0, 0)),
        ],
        out_specs=pl.BlockSpec((rows, vw), lambda h, t: (t, h)),
        out_shape=jax.ShapeDtypeStruct((m, GDN_VAL_DIM), BF16),
        scratch_shapes=[
            pltpu.VMEM((2 * GDN_KH_PER_STEP, HEAD_DIM, HEAD_DIM), F32),
            pltpu.VMEM((CARRY_ROWS, kw), F32),
            pltpu.VMEM((CARRY_ROWS, kw), F32),
            pltpu.VMEM((CARRY_ROWS, vw), F32),
        ],
        compiler_params=_params("arbitrary", "arbitrary"),
        name="gdn_mixer",
    )(proj, proj, proj, proj, conv_w, conv_w, conv_w, g_row, g_col, b_col,
      head_norm.reshape(1, HEAD_DIM))


def _row_broadcast(x, first, period, rows):
    parts = [jnp.broadcast_to(x[r:r + 1], (period, x.shape[1])) for r in range(first, rows, period)]
    return jnp.concatenate(parts, axis=0)


def _hgrn_kernel(q_ref, f_ref, i_ref, g_ref, lb_ref, hn_ref, o_ref, s_ref):
    rows = q_ref.shape[0]
    n_groups = rows // GROUP
    n_chunks = rows // CHUNK
    per_group = GROUP // CHUNK

    @pl.when(pl.program_id(1) == 0)
    def _():
        s_ref[...] = jnp.zeros_like(s_ref)

    lbs = lb_ref[...]
    mx = jnp.max(lbs, axis=0, keepdims=True)
    ex = jnp.exp(lbs - mx)
    sm = ex / jnp.sum(ex, axis=0, keepdims=True)
    lb = (sm[0:1] + sm[1:2]) - sm[0:1]

    q = _silu(q_ref[...].astype(F32))
    fl = f_ref[...].astype(F32)
    v = i_ref[...].astype(F32)
    en = jnp.exp(-jnp.abs(fl))
    rc = 1.0 / (1.0 + en)
    pos = fl >= 0.0
    sig = jnp.where(pos, rc, en * rc)
    nsig = jnp.where(pos, en * rc, rc)
    logf = jnp.log(lb + (1.0 - lb) * sig)
    kk = (1.0 - lb) * nsig

    rowi = lax.broadcasted_iota(jnp.int32, (rows, HEAD_DIM), 0)
    row = lax.broadcasted_iota(jnp.int32, (GROUP, GROUP), 0)
    col = lax.broadcasted_iota(jnp.int32, (GROUP, GROUP), 1)

    tri = (((row // CHUNK) == (col // CHUNK)) & (row >= col)).astype(BF16)
    logf2 = logf * LOG2E
    hi = logf2.astype(BF16)
    rest = logf2 - hi.astype(F32)
    mid = rest.astype(BF16)
    lo = (rest - mid.astype(F32)).astype(BF16)
    pieces = jnp.concatenate([hi, mid, lo], axis=1)

    def mid_range(s):
        mid_row = (row // (2 * s)) * (2 * s) + (s - 1)
        return ((col > mid_row) & (col <= row)) | ((col > row) & (col <= mid_row))

    small_ranges = jnp.concatenate([mid_range(s).astype(BF16) for s in HGRN_SMALL_HALVES], axis=0)
    b_parts, small_parts = [], []
    for g in range(n_groups):
        pg = pieces[g * GROUP:(g + 1) * GROUP]
        s3 = jnp.dot(tri, pg, preferred_element_type=F32)
        b_parts.append(s3[:, :HEAD_DIM] + s3[:, HEAD_DIM:2 * HEAD_DIM] + s3[:, 2 * HEAD_DIM:])
        s2 = jnp.dot(small_ranges, pg[:, :2 * HEAD_DIM], preferred_element_type=F32)
        small_parts.append(s2[:, :HEAD_DIM] + s2[:, HEAD_DIM:])
    b = jnp.concatenate(b_parts, axis=0)
    b_last = _row_broadcast(b, CHUNK - 1, CHUNK, rows)
    qd = q * jnp.exp2(b)
    kd = kk * jnp.exp2(b_last - b)

    halves = HGRN_BIG_HALVES + HGRN_SMALL_HALVES
    level_x = []
    for s in HGRN_BIG_HALVES:
        b_mid = _row_broadcast(b, s - 1, 2 * s, rows)
        lower = (rowi % (2 * s)) >= s
        level_x.append(jnp.where(lower, q, kk) * jnp.exp2(-jnp.abs(b - b_mid)))
    for n, s in enumerate(HGRN_SMALL_HALVES):
        expo = jnp.concatenate([p[n * GROUP:(n + 1) * GROUP] for p in small_parts], axis=0)
        lower = (rowi % (2 * s)) >= s
        level_x.append(jnp.where(lower, q, kk) * jnp.exp2(expo))

    def rot8(x, d):
        return pltpu.roll(x.reshape(rows // 8, 8, HEAD_DIM), d, 1).reshape(rows, HEAD_DIM)

    diags = [jnp.sum(q * kk, axis=1, keepdims=True)]
    for d in range(1, HGRN_DIAG):
        e = jnp.exp2(jnp.minimum(b - rot8(b, d), 0.0))
        diags.append(jnp.sum(q * rot8(kk, d) * e, axis=1, keepdims=True))

    def grp(x, g):
        return x[g * GROUP:(g + 1) * GROUP]

    def chk(x, c):
        return x[c * CHUNK:(c + 1) * CHUNK]

    level_mm = [[_mm_nt(grp(x, g), grp(x, g)) for x in level_x] for g in range(n_groups)]
    kv = [_mm_tn(chk(v, c), chk(kd, c)) for c in range(n_chunks)]

    scores = []
    for g in range(n_groups):
        sc = jnp.zeros((GROUP, GROUP), F32)
        for s, mm in zip(halves, level_mm[g]):
            pair = ((row // (2 * s)) == (col // (2 * s))) & ((row % (2 * s)) >= s) & ((col % (2 * s)) < s)
            sc = jnp.where(pair, mm, sc)
        same_diag = (row // HGRN_DIAG) == (col // HGRN_DIAG)
        for d in range(HGRN_DIAG):
            sc = jnp.where(same_diag & (row - col == d), grp(diags[d], g), sc)
        scores.append(sc)

    state_t = s_ref[...]
    states = []
    for c in range(n_chunks):
        states.append(state_t)
        f_last = jnp.exp2(b[(c + 1) * CHUNK - 1:(c + 1) * CHUNK])
        state_t = state_t * f_last + kv[c]
    s_ref[...] = state_t

    intra = [_mm(scores[g], grp(v, g)) for g in range(n_groups)]
    inter = [_mm_nt(chk(qd, c), states[c]) for c in range(n_chunks)]
    hn = hn_ref[...]
    for c in range(n_chunks):
        g, i = divmod(c, per_group)
        o = inter[c] + intra[g][i * CHUNK:(i + 1) * CHUNK]
        gate = g_ref[c * CHUNK:(c + 1) * CHUNK, :].astype(F32)
        o_ref[c * CHUNK:(c + 1) * CHUNK, :] = _gated_rmsnorm(o, hn, gate).astype(o_ref.dtype)


def _hgrn_mixer(proj, lower_bounds, head_norm):
    m = proj.shape[0]
    rows = HGRN_ROWS
    nb = HGRN_DIM // HEAD_DIM
    return pl.pallas_call(
        _hgrn_kernel,
        grid=(HGRN_HEADS, m // rows),
        in_specs=[
            pl.BlockSpec((rows, HEAD_DIM), lambda h, t: (t, h)),
            pl.BlockSpec((rows, HEAD_DIM), lambda h, t: (t, nb + h)),
            pl.BlockSpec((rows, HEAD_DIM), lambda h, t: (t, 2 * nb + h)),
            pl.BlockSpec((rows, HEAD_DIM), lambda h, t: (t, 3 * nb + h)),
            pl.BlockSpec((2, HEAD_DIM), lambda h, t: (0, h)),
            pl.BlockSpec((1, HEAD_DIM), lambda h, t: (0, 0)),
        ],
        out_specs=pl.BlockSpec((rows, HEAD_DIM), lambda h, t: (t, h)),
        out_shape=jax.ShapeDtypeStruct((m, HGRN_DIM), BF16),
        scratch_shapes=[pltpu.VMEM((HEAD_DIM, HEAD_DIM), F32)],
        compiler_params=_params("arbitrary", "arbitrary"),
        name="hgrn_mixer",
    )(proj, proj, proj, proj, lower_bounds, head_norm.reshape(1, HEAD_DIM))


def _ffn(h, layer, norm_w, w_gate, w_up, w_down, hn=None):
    if hn is None:
        hn = _rmsnorm(h, norm_w[layer], BF16)
    act = _swiglu_up(hn, w_gate, w_up, layer, tm=1024, tn=512)
    return _proj_residual(act, w_down, layer, h, tm=512, tn=512)


def kernel(x, gdn_norm, gdn_w_in, gdn_conv, gdn_a_log, gdn_dt_bias, gdn_head_norm, gdn_w_out,
           hgrn_norm, hgrn_w_in, hgrn_lower_bounds, hgrn_head_norm, hgrn_w_out,
           ffn_norm, ffn_w_gate, ffn_w_up, ffn_w_down, final_norm):
    assert x.shape == (1, SEQ, D_MODEL)
    h = x.reshape(SEQ, D_MODEL)

    hn = _rmsnorm(h, gdn_norm[0], BF16)
    w_in_t = jnp.swapaxes(gdn_w_in, 1, 2)
    proj = _proj(hn, w_in_t, 0, GDN_MAIN_DIM, tm=1024, tn=1024, transposed=True)
    w_ba_t = w_in_t[0, GDN_MAIN_DIM:]
    g_row, b_row = _gdn_gates(hn, w_ba_t, gdn_a_log[0], gdn_dt_bias[0])
    mixed = _gdn_mixer(proj, gdn_conv[0], g_row, g_row.T, b_row.T, gdn_head_norm[0])
    h = _proj_residual(mixed, gdn_w_out, 0, h, tm=1024, tn=512)
    h = _ffn(h, 0, ffn_norm, ffn_w_gate, ffn_w_up, ffn_w_down)

    hn = _rmsnorm(h, hgrn_norm[0], BF16)
    proj = _proj(hn, hgrn_w_in, 0, 4 * HGRN_DIM, tm=1024, tn=1024)
    mixed = _hgrn_mixer(proj, hgrn_lower_bounds, hgrn_head_norm[0])
    h, hn = _proj_residual_norm(mixed, hgrn_w_out, 0, h, ffn_norm[1], tm=512)
    h = _ffn(h, 1, ffn_norm, ffn_w_gate, ffn_w_up, ffn_w_down, hn=hn)

    return _rmsnorm(h, final_norm, F32).reshape(1, SEQ, D_MODEL)
```

```python
import jax
import jax.numpy as jnp
from jax import lax
from jax.experimental import pallas as pl
from jax.experimental.pallas import tpu as pltpu

F32 = jnp.float32
BF16 = jnp.bfloat16

D_MODEL = 2048
SEQ = 8192
CHUNK = 64
HEAD_DIM = 128
EPS = 1e-6
LOG2E = 1.4426950408889634

GDN_K_HEADS = 16
GDN_V_HEADS = 32
GDN_KEY_DIM = GDN_K_HEADS * HEAD_DIM
GDN_VAL_DIM = GDN_V_HEADS * HEAD_DIM
GDN_CONV_DIM = 2 * GDN_KEY_DIM + GDN_VAL_DIM
GDN_MAIN_DIM = GDN_CONV_DIM + GDN_VAL_DIM
GDN_CONV_TAPS = 4

HGRN_HEADS = 16
HGRN_DIM = HGRN_HEADS * HEAD_DIM

VMEM_LIMIT_BYTES = 56 * 1024 * 1024
CARRY_ROWS = 8

GROUP = 2 * CHUNK
GDN_KH_PER_STEP = 4
GDN_ROWS = 512
HGRN_ROWS = 2048
HGRN_BIG_HALVES = (32, 16, 8)
HGRN_SMALL_HALVES = (4, 2)
HGRN_DIAG = 2


def _params(*semantics):
    return pltpu.CompilerParams(dimension_semantics=semantics, vmem_limit_bytes=VMEM_LIMIT_BYTES)


def _mm(a, b):
    return jnp.dot(a.astype(BF16), b.astype(BF16), preferred_element_type=F32)


def _mm_nt(a, b):
    return lax.dot_general(a.astype(BF16), b.astype(BF16), (((1,), (1,)), ((), ())),
                           preferred_element_type=F32)


def _mm_tn(a, b):
    return lax.dot_general(a.astype(BF16), b.astype(BF16), (((0,), (0,)), ((), ())),
                           preferred_element_type=F32)


def _sigmoid(x):
    return 1.0 / (1.0 + jnp.exp(-x))


def _silu(x):
    return x * _sigmoid(x)


def _rmsnorm_kernel(x_ref, w_ref, o_ref):
    x = x_ref[...]
    y = x * lax.rsqrt(jnp.mean(x * x, axis=-1, keepdims=True) + EPS)
    o_ref[...] = (y * w_ref[...]).astype(o_ref.dtype)


def _rmsnorm(x, w, out_dtype, rows=1024):
    m, d = x.shape
    return pl.pallas_call(
        _rmsnorm_kernel,
        grid=(m // rows,),
        in_specs=[pl.BlockSpec((rows, d), lambda i: (i, 0)),
                  pl.BlockSpec((1, d), lambda i: (0, 0))],
        out_specs=pl.BlockSpec((rows, d), lambda i: (i, 0)),
        out_shape=jax.ShapeDtypeStruct((m, d), out_dtype),
        compiler_params=_params("arbitrary"),
        name="rmsnorm",
    )(x, w.reshape(1, d))


def _weight_spec(k, tn, layer, first_tile=0, transposed=False, single_buffer=False):
    mode = dict(pipeline_mode=pl.Buffered(1)) if single_buffer else {}
    if transposed:
        return pl.BlockSpec((None, tn, k), lambda j, i: (layer, first_tile + j, 0), **mode)
    return pl.BlockSpec((None, k, tn), lambda j, i: (layer, 0, first_tile + j), **mode)


def _load_weight(w_ref, wb_ref):
    w = w_ref[...]
    if w.shape != wb_ref.shape:
        w = w.T
    wb_ref[...] = w.astype(BF16)


def _proj_kernel(x_ref, w_ref, o_ref, wb_ref):
    @pl.when(pl.program_id(1) == 0)
    def _():
        _load_weight(w_ref, wb_ref)

    o_ref[...] = jnp.dot(x_ref[...], wb_ref[...], preferred_element_type=F32).astype(o_ref.dtype)


def _proj(x, w, layer, n_cols, tm, tn, first_col=0, transposed=False):
    m, k = x.shape
    return pl.pallas_call(
        _proj_kernel,
        grid=(n_cols // tn, m // tm),
        in_specs=[pl.BlockSpec((tm, k), lambda j, i: (i, 0)),
                  _weight_spec(k, tn, layer, first_col // tn, transposed)],
        out_specs=pl.BlockSpec((tm, tn), lambda j, i: (i, j)),
        out_shape=jax.ShapeDtypeStruct((m, n_cols), BF16),
        scratch_shapes=[pltpu.VMEM((k, tn), BF16)],
        compiler_params=_params("arbitrary", "arbitrary"),
        name="proj",
    )(x, w)


def _proj_residual_kernel(x_ref, w_ref, r_ref, o_ref, wb_ref):
    @pl.when(pl.program_id(1) == 0)
    def _():
        wb_ref[...] = w_ref[...].astype(BF16)

    o_ref[...] = r_ref[...] + jnp.dot(x_ref[...], wb_ref[...], preferred_element_type=F32)


def _proj_residual(x, w, layer, res, tm, tn, single_buffer=False):
    m, k = x.shape
    n = w.shape[2]
    return pl.pallas_call(
        _proj_residual_kernel,
        grid=(n // tn, m // tm),
        in_specs=[pl.BlockSpec((tm, k), lambda j, i: (i, 0)),
                  _weight_spec(k, tn, layer, single_buffer=single_buffer),
                  pl.BlockSpec((tm, tn), lambda j, i: (i, j))],
        out_specs=pl.BlockSpec((tm, tn), lambda j, i: (i, j)),
        out_shape=jax.ShapeDtypeStruct((m, n), F32),
        scratch_shapes=[pltpu.VMEM((k, tn), BF16)],
        compiler_params=_params("arbitrary", "arbitrary"),
        name="proj_residual",
    )(x, w, res)


def _proj_residual_norm_kernel(x_ref, w_ref, r_ref, nw_ref, o_ref, on_ref, wb_ref):
    @pl.when(pl.program_id(0) == 0)
    def _():
        wb_ref[...] = w_ref[...].astype(BF16)

    h = r_ref[...] + jnp.dot(x_ref[...], wb_ref[...], preferred_element_type=F32)
    o_ref[...] = h
    y = h * lax.rsqrt(jnp.mean(h * h, axis=-1, keepdims=True) + EPS)
    on_ref[...] = (y * nw_ref[...]).astype(on_ref.dtype)


def _proj_residual_norm(x, w, layer, res, norm_w, tm):
    m, k = x.shape
    n = w.shape[2]
    return pl.pallas_call(
        _proj_residual_norm_kernel,
        grid=(m // tm,),
        in_specs=[pl.BlockSpec((tm, k), lambda i: (i, 0)),
                  pl.BlockSpec((None, k, n), lambda i: (layer, 0, 0), pipeline_mode=pl.Buffered(1)),
                  pl.BlockSpec((tm, n), lambda i: (i, 0)),
                  pl.BlockSpec((1, n), lambda i: (0, 0))],
        out_specs=[pl.BlockSpec((tm, n), lambda i: (i, 0)), pl.BlockSpec((tm, n), lambda i: (i, 0))],
        out_shape=[jax.ShapeDtypeStruct((m, n), F32), jax.ShapeDtypeStruct((m, n), BF16)],
        scratch_shapes=[pltpu.VMEM((k, n), BF16)],
        compiler_params=_params("arbitrary"),
        name="proj_residual_norm",
    )(x, w, res, norm_w.reshape(1, n))


def _swiglu_up_kernel(x_ref, wg_ref, wu_ref, o_ref, wgb_ref, wub_ref):
    @pl.when(pl.program_id(1) == 0)
    def _():
        wgb_ref[...] = wg_ref[...].astype(BF16)
        wub_ref[...] = wu_ref[...].astype(BF16)

    x = x_ref[...]
    g = jnp.dot(x, wgb_ref[...], preferred_element_type=F32)
    u = jnp.dot(x, wub_ref[...], preferred_element_type=F32)
    o_ref[...] = (_silu(g) * u).astype(o_ref.dtype)


def _swiglu_up(x, w_gate, w_up, layer, tm, tn):
    m, k = x.shape
    n = w_gate.shape[2]
    return pl.pallas_call(
        _swiglu_up_kernel,
        grid=(n // tn, m // tm),
        in_specs=[pl.BlockSpec((tm, k), lambda j, i: (i, 0)), _weight_spec(k, tn, layer),
                  _weight_spec(k, tn, layer)],
        out_specs=pl.BlockSpec((tm, tn), lambda j, i: (i, j)),
        out_shape=jax.ShapeDtypeStruct((m, n), BF16),
        scratch_shapes=[pltpu.VMEM((k, tn), BF16), pltpu.VMEM((k, tn), BF16)],
        compiler_params=_params("arbitrary", "arbitrary"),
        name="swiglu_up",
    )(x, w_gate, w_up)


def _gdn_gate_kernel(x_ref, w_ref, alog_ref, dtb_ref, g_ref, beta_ref):
    ba = _mm_nt(w_ref[...], x_ref[...])
    nh = GDN_V_HEADS
    beta_ref[...] = _sigmoid(ba[:nh])
    a = ba[nh:] + dtb_ref[...]
    softplus = jnp.maximum(a, 0.0) + jnp.log1p(jnp.exp(-jnp.abs(a)))
    g = (-LOG2E * jnp.exp(alog_ref[...])) * softplus
    pos = lax.broadcasted_iota(jnp.int32, g.shape, 1) % CHUNK
    shift = 1
    while shift < CHUNK:
        g = g + jnp.where(pos >= shift, pltpu.roll(g, shift, 1), 0.0)
        shift *= 2
    g_ref[...] = g


def _gdn_gates(x, w_ba_t, a_log, dt_bias, rows=512):
    m, k = x.shape
    nh = GDN_V_HEADS
    return pl.pallas_call(
        _gdn_gate_kernel,
        grid=(m // rows,),
        in_specs=[pl.BlockSpec((rows, k), lambda i: (i, 0)),
                  pl.BlockSpec((2 * nh, k), lambda i: (0, 0)),
                  pl.BlockSpec((nh, 1), lambda i: (0, 0)),
                  pl.BlockSpec((nh, 1), lambda i: (0, 0))],
        out_specs=[pl.BlockSpec((nh, rows), lambda i: (0, i)),
                   pl.BlockSpec((nh, rows), lambda i: (0, i))],
        out_shape=[jax.ShapeDtypeStruct((nh, m), F32), jax.ShapeDtypeStruct((nh, m), F32)],
        compiler_params=_params("arbitrary"),
        name="gdn_gates",
    )(x, w_ba_t, a_log.reshape(nh, 1), dt_bias.reshape(nh, 1))


def _conv_silu(x_ref, carry_ref, w_ref):
    x = x_ref[...].astype(F32)
    rows, width = x.shape
    xe = jnp.concatenate([carry_ref[...], x], axis=0).reshape(rows // 8 + 1, 8, width)
    in_tile = lax.broadcasted_iota(jnp.int32, (1, 8, width), 1)
    w = w_ref[...]
    taps = GDN_CONV_TAPS
    y = w[taps - 1:taps] * x
    for j in range(1, taps):
        r = pltpu.roll(xe, j, 1)
        shifted = jnp.where(in_tile < j, r[:-1], r[1:]).reshape(rows, width)
        y = y + w[taps - 1 - j:taps - j] * shifted
    carry_ref[...] = x[rows - CARRY_ROWS:]
    return _silu(y)


def _l2norm(x):
    return x * lax.rsqrt(jnp.sum(x * x, axis=-1, keepdims=True) + 1e-6)


def _gated_rmsnorm(o, w, z):
    y = o * lax.rsqrt(jnp.mean(o * o, axis=-1, keepdims=True) + EPS)
    return y * w * _silu(z)


def _unit_lower_inverses(a_list, row, col):
    eye = (row == col).astype(F32)
    same16 = (row // 16) == (col // 16)
    same32 = (row // 32) == (col // 32)
    off32 = same32 & jnp.logical_not(same16)
    off64 = jnp.logical_not(same32)
    ad = [jnp.where(same16, a, 0.0) for a in a_list]
    n = row.shape[0]
    p = [_mm(x, x) for x in ad]
    t = [eye - x for x in ad]
    for _ in range(2):
        both = [_mm(jnp.concatenate([x, y], axis=0), y) for x, y in zip(t, p)]
        t = [x + b[:n] for x, b in zip(t, both)]
        p = [b[n:] for b in both]
    t = [x + _mm(x, y) for x, y in zip(t, p)]
    for mask in (off32, off64):
        y = [_mm(jnp.where(mask, a, 0.0), x) for a, x in zip(a_list, t)]
        t = [x - _mm(x, yy) for x, yy in zip(t, y)]
    return t


def _gdn_kernel(q_ref, k_ref, v_ref, z_ref, wq_ref, wk_ref, wv_ref, grow_ref,
                gcol_ref, bcol_ref, hn_ref, o_ref, s_ref, cq_ref, ck_ref, cv_ref):
    hb = pl.program_id(0)
    rows = q_ref.shape[0]
    n_groups = rows // GROUP
    n_chunks = rows // CHUNK
    per_group = GROUP // CHUNK

    @pl.when(pl.program_id(1) == 0)
    def _():
        s_ref[...] = jnp.zeros_like(s_ref)
        cq_ref[...] = jnp.zeros_like(cq_ref)
        ck_ref[...] = jnp.zeros_like(ck_ref)
        cv_ref[...] = jnp.zeros_like(cv_ref)

    q_all = _conv_silu(q_ref, cq_ref, wq_ref)
    k_all = _conv_silu(k_ref, ck_ref, wk_ref)
    v_all = _conv_silu(v_ref, cv_ref, wv_ref)
    hn = hn_ref[...]

    def lanes(x, i):
        return x[:, i * HEAD_DIM:(i + 1) * HEAD_DIM]

    qs = [_l2norm(lanes(q_all, j)) * (HEAD_DIM ** -0.5) for j in range(GDN_KH_PER_STEP)]
    ks = [_l2norm(lanes(k_all, j)) for j in range(GDN_KH_PER_STEP)]

    row = lax.broadcasted_iota(jnp.int32, (GROUP, GROUP), 0)
    col = lax.broadcasted_iota(jnp.int32, (GROUP, GROUP), 1)
    same_chunk = (row // CHUNK) == (col // CHUNK)
    causal = (row >= col) & same_chunk
    strict = row > col

    heads = [(j, e) for j in range(GDN_KH_PER_STEP) for e in range(2)]
    head_lane = lax.broadcasted_iota(jnp.int32, gcol_ref.shape, 1)
    gcol_all = gcol_ref[...]
    bcol_all = bcol_ref[...]
    g_col, b_col, g_row = [], [], []
    for j, e in heads:
        head = 2 * (GDN_KH_PER_STEP * hb + j) + e
        sel = head_lane == head
        g_col.append(jnp.sum(jnp.where(sel, gcol_all, 0.0), axis=1, keepdims=True))
        b_col.append(jnp.sum(jnp.where(sel, bcol_all, 0.0), axis=1, keepdims=True))
        g_row.append(grow_ref[pl.ds(head, 1), :])

    def grp(x, g):
        return x[g * GROUP:(g + 1) * GROUP]

    kq = [[_mm_nt(jnp.concatenate([grp(qs[j], g), grp(ks[j], g)], axis=0), grp(ks[j], g))
           for g in range(n_groups)] for j in range(GDN_KH_PER_STEP)]

    items = [(h, g) for h in range(len(heads)) for g in range(n_groups)]
    a_list, aqk_list, rhs_list, qd_list, kd_list = [], [], [], [], []
    for h, g in items:
        j, e = heads[h]
        gc = grp(g_col[h], g)
        bc = grp(b_col[h], g)
        gr = g_row[h][:, g * GROUP:(g + 1) * GROUP]
        decay = jnp.exp2(jnp.where(causal, gc - gr, -jnp.inf))
        a_list.append(jnp.where(strict, bc * kq[j][g][GROUP:] * decay, 0.0))
        aqk_list.append(kq[j][g][:GROUP] * decay)
        eg = jnp.exp2(gc)
        kg = grp(ks[j], g)
        vg = grp(lanes(v_all, 2 * j + e), g)
        rhs_list.append(jnp.concatenate([vg * bc, kg * (bc * eg)], axis=1))
        g_last = jnp.concatenate(
            [jnp.broadcast_to(gc[(i + 1) * CHUNK - 1:(i + 1) * CHUNK], (CHUNK, 1)) for i in range(per_group)],
            axis=0)
        qd_list.append(grp(qs[j], g) * eg)
        kd_list.append(kg * jnp.exp2(g_last - gc))

    t_list = _unit_lower_inverses(a_list, row, col)
    uw_list = [_mm(t, r) for t, r in zip(t_list, rhs_list)]

    states = [s_ref[h] for h in range(len(heads))]
    for c in range(n_chunks):
        g, i = divmod(c, per_group)
        sl = slice(i * CHUNK, (i + 1) * CHUNK)
        idx = [h * n_groups + g for h in range(len(heads))]
        ws_qs = [_mm(jnp.concatenate([uw_list[n][sl, HEAD_DIM:], qd_list[n][sl]], axis=0), states[h])
                 for h, n in enumerate(idx)]
        v_new = [uw_list[n][sl, :HEAD_DIM] - r[:CHUNK] for n, r in zip(idx, ws_qs)]
        intra = [_mm(aqk_list[n][sl, i * CHUNK:(i + 1) * CHUNK], vn) for n, vn in zip(idx, v_new)]
        upd = [_mm_tn(kd_list[n][sl], vn) for n, vn in zip(idx, v_new)]
        for h in range(len(heads)):
            j, e = heads[h]
            g_last = g_col[h][(c + 1) * CHUNK - 1:(c + 1) * CHUNK]
            states[h] = states[h] * jnp.exp2(g_last) + upd[h]
            o = ws_qs[h][CHUNK:] + intra[h]
            z = z_ref[c * CHUNK:(c + 1) * CHUNK, (2 * j + e) * HEAD_DIM:(2 * j + e + 1) * HEAD_DIM].astype(F32)
            o_ref[c * CHUNK:(c + 1) * CHUNK, (2 * j + e) * HEAD_DIM:(2 * j + e + 1) * HEAD_DIM] = (
                _gated_rmsnorm(o, hn, z).astype(o_ref.dtype))
    for h in range(len(heads)):
        s_ref[h] = states[h]


def _gdn_mixer(proj, conv_w, g_row, g_col, b_col, head_norm):
    m = proj.shape[0]
    rows = GDN_ROWS
    nh = GDN_V_HEADS
    kw = GDN_KH_PER_STEP * HEAD_DIM
    vw = 2 * kw
    kb = GDN_KEY_DIM // kw
    vb = 2 * GDN_KEY_DIM // vw
    zb = GDN_CONV_DIM // vw
    return pl.pallas_call(
        _gdn_kernel,
        grid=(GDN_K_HEADS // GDN_KH_PER_STEP, m // rows),
        in_specs=[
            pl.BlockSpec((rows, kw), lambda h, t: (t, h)),
            pl.BlockSpec((rows, kw), lambda h, t: (t, kb + h)),
            pl.BlockSpec((rows, vw), lambda h, t: (t, vb + h)),
            pl.BlockSpec((rows, vw), lambda h, t: (t, zb + h)),
            pl.BlockSpec((GDN_CONV_TAPS, kw), lambda h, t: (0, h)),
            pl.BlockSpec((GDN_CONV_TAPS, kw), lambda h, t: (0, kb + h)),
            pl.BlockSpec((GDN_CONV_TAPS, vw), lambda h, t: (0, vb + h)),
            pl.BlockSpec((nh, rows), lambda h, t: (0, t)),
            pl.BlockSpec((rows, nh), lambda h, t: (t, 0)),
            pl.BlockSpec((rows, nh), lambda h, t: (t, 0)),
            pl.BlockSpec((1, HEAD_DIM), lambda h, t: (0, 0)),
        ],
        out_specs=pl.BlockSpec((rows, vw), lambda h, t: (t, h)),
        out_shape=jax.ShapeDtypeStruct((m, GDN_VAL_DIM), BF16),
        scratch_shapes=[
            pltpu.VMEM((2 * GDN_KH_PER_STEP, HEAD_DIM, HEAD_DIM), F32),
            pltpu.VMEM((CARRY_ROWS, kw), F32),
            pltpu.VMEM((CARRY_ROWS, kw), F32),
            pltpu.VMEM((CARRY_ROWS, vw), F32),
        ],
        compiler_params=_params("arbitrary", "arbitrary"),
        name="gdn_mixer",
    )(proj, proj, proj, proj, conv_w, conv_w, conv_w, g_row, g_col, b_col,
      head_norm.reshape(1, HEAD_DIM))


def _row_broadcast(x, first, period, rows):
    parts = [jnp.broadcast_to(x[r:r + 1], (period, x.shape[1])) for r in range(first, rows, period)]
    return jnp.concatenate(parts, axis=0)


def _hgrn_kernel(q_ref, f_ref, i_ref, g_ref, lb_ref, hn_ref, o_ref, s_ref):
    rows = q_ref.shape[0]
    n_groups = rows // GROUP
    n_chunks = rows // CHUNK
    per_group = GROUP // CHUNK

    @pl.when(pl.program_id(1) == 0)
    def _():
        s_ref[...] = jnp.zeros_like(s_ref)

    lbs = lb_ref[...]
    mx = jnp.max(lbs, axis=0, keepdims=True)
    ex = jnp.exp(lbs - mx)
    sm = ex / jnp.sum(ex, axis=0, keepdims=True)
    lb = (sm[0:1] + sm[1:2]) - sm[0:1]

    q = _silu(q_ref[...].astype(F32))
    fl = f_ref[...].astype(F32)
    v = i_ref[...].astype(F32)
    en = jnp.exp(-jnp.abs(fl))
    rc = 1.0 / (1.0 + en)
    pos = fl >= 0.0
    sig = jnp.where(pos, rc, en * rc)
    nsig = jnp.where(pos, en * rc, rc)
    logf = jnp.log(lb + (1.0 - lb) * sig)
    kk = (1.0 - lb) * nsig

    rowi = lax.broadcasted_iota(jnp.int32, (rows, HEAD_DIM), 0)
    row = lax.broadcasted_iota(jnp.int32, (GROUP, GROUP), 0)
    col = lax.broadcasted_iota(jnp.int32, (GROUP, GROUP), 1)

    tri = (((row // CHUNK) == (col // CHUNK)) & (row >= col)).astype(BF16)
    logf2 = logf * LOG2E
    hi = logf2.astype(BF16)
    rest = logf2 - hi.astype(F32)
    mid = rest.astype(BF16)
    lo = (rest - mid.astype(F32)).astype(BF16)
    pieces = jnp.concatenate([hi, mid, lo], axis=1)

    def mid_range(s):
        mid_row = (row // (2 * s)) * (2 * s) + (s - 1)
        return ((col > mid_row) & (col <= row)) | ((col > row) & (col <= mid_row))

    small_ranges = jnp.concatenate([mid_range(s).astype(BF16) for s in HGRN_SMALL_HALVES], axis=0)
    b_parts, small_parts = [], []
    for g in range(n_groups):
        pg = pieces[g * GROUP:(g + 1) * GROUP]
        s3 = jnp.dot(tri, pg, preferred_element_type=F32)
        b_parts.append(s3[:, :HEAD_DIM] + s3[:, HEAD_DIM:2 * HEAD_DIM] + s3[:, 2 * HEAD_DIM:])
        s2 = jnp.dot(small_ranges, pg[:, :2 * HEAD_DIM], preferred_element_type=F32)
        small_parts.append(s2[:, :HEAD_DIM] + s2[:, HEAD_DIM:])
    b = jnp.concatenate(b_parts, axis=0)
    b_last = _row_broadcast(b, CHUNK - 1, CHUNK, rows)
    qd = q * jnp.exp2(b)
    kd = kk * jnp.exp2(b_last - b)

    halves = HGRN_BIG_HALVES + HGRN_SMALL_HALVES
    level_x = []
    for s in HGRN_BIG_HALVES:
        b_mid = _row_broadcast(b, s - 1, 2 * s, rows)
        lower = (rowi % (2 * s)) >= s
        level_x.append(jnp.where(lower, q, kk) * jnp.exp2(-jnp.abs(b - b_mid)))
    for n, s in enumerate(HGRN_SMALL_HALVES):
        expo = jnp.concatenate([p[n * GROUP:(n + 1) * GROUP] for p in small_parts], axis=0)
        lower = (rowi % (2 * s)) >= s
        level_x.append(jnp.where(lower, q, kk) * jnp.exp2(expo))

    def rot8(x, d):
        return pltpu.roll(x.reshape(rows // 8, 8, HEAD_DIM), d, 1).reshape(rows, HEAD_DIM)

    diags = [jnp.sum(q * kk, axis=1, keepdims=True)]
    for d in range(1, HGRN_DIAG):
        e = jnp.exp2(jnp.minimum(b - rot8(b, d), 0.0))
        diags.append(jnp.sum(q * rot8(kk, d) * e, axis=1, keepdims=True))

    def grp(x, g):
        return x[g * GROUP:(g + 1) * GROUP]

    def chk(x, c):
        return x[c * CHUNK:(c + 1) * CHUNK]

    level_mm = [[_mm_nt(grp(x, g), grp(x, g)) for x in level_x] for g in range(n_groups)]
    kv = [_mm_tn(chk(v, c), chk(kd, c)) for c in range(n_chunks)]

    scores = []
    for g in range(n_groups):
        sc = jnp.zeros((GROUP, GROUP), F32)
        for s, mm in zip(halves, level_mm[g]):
            pair = ((row // (2 * s)) == (col // (2 * s))) & ((row % (2 * s)) >= s) & ((col % (2 * s)) < s)
            sc = jnp.where(pair, mm, sc)
        same_diag = (row // HGRN_DIAG) == (col // HGRN_DIAG)
        for d in range(HGRN_DIAG):
            sc = jnp.where(same_diag & (row - col == d), grp(diags[d], g), sc)
        scores.append(sc)

    state_t = s_ref[...]
    states = []
    for c in range(n_chunks):
        states.append(state_t)
        f_last = jnp.exp2(b[(c + 1) * CHUNK - 1:(c + 1) * CHUNK])
        state_t = state_t * f_last + kv[c]
    s_ref[...] = state_t

    intra = [_mm(scores[g], grp(v, g)) for g in range(n_groups)]
    inter = [_mm_nt(chk(qd, c), states[c]) for c in range(n_chunks)]
    hn = hn_ref[...]
    for c in range(n_chunks):
        g, i = divmod(c, per_group)
        o = inter[c] + intra[g][i * CHUNK:(i + 1) * CHUNK]
        gate = g_ref[c * CHUNK:(c + 1) * CHUNK, :].astype(F32)
        o_ref[c * CHUNK:(c + 1) * CHUNK, :] = _gated_rmsnorm(o, hn, gate).astype(o_ref.dtype)


def _hgrn_mixer(proj, lower_bounds, head_norm):
    m = proj.shape[0]
    rows = HGRN_ROWS
    nb = HGRN_DIM // HEAD_DIM
    return pl.pallas_call(
        _hgrn_kernel,
        grid=(HGRN_HEADS, m // rows),
        in_specs=[
            pl.BlockSpec((rows, HEAD_DIM), lambda h, t: (t, h)),
            pl.BlockSpec((rows, HEAD_DIM), lambda h, t: (t, nb + h)),
            pl.BlockSpec((rows, HEAD_DIM), lambda h, t: (t, 2 * nb + h)),
            pl.BlockSpec((rows, HEAD_DIM), lambda h, t: (t, 3 * nb + h)),
            pl.BlockSpec((2, HEAD_DIM), lambda h, t: (0, h)),
            pl.BlockSpec((1, HEAD_DIM), lambda h, t: (0, 0)),
        ],
        out_specs=pl.BlockSpec((rows, HEAD_DIM), lambda h, t: (t, h)),
        out_shape=jax.ShapeDtypeStruct((m, HGRN_DIM), BF16),
        scratch_shapes=[pltpu.VMEM((HEAD_DIM, HEAD_DIM), F32)],
        compiler_params=_params("arbitrary", "arbitrary"),
        name="hgrn_mixer",
    )(proj, proj, proj, proj, lower_bounds, head_norm.reshape(1, HEAD_DIM))


def _ffn(h, layer, norm_w, w_gate, w_up, w_down, hn=None):
    if hn is None:
        hn = _rmsnorm(h, norm_w[layer], BF16)
    act = _swiglu_up(hn, w_gate, w_up, layer, tm=2048, tn=512)
    return _proj_residual(act, w_down, layer, h, tm=1024, tn=512, single_buffer=True)


def kernel(x, gdn_norm, gdn_w_in, gdn_conv, gdn_a_log, gdn_dt_bias, gdn_head_norm, gdn_w_out,
           hgrn_norm, hgrn_w_in, hgrn_lower_bounds, hgrn_head_norm, hgrn_w_out,
           ffn_norm, ffn_w_gate, ffn_w_up, ffn_w_down, final_norm):
    assert x.shape == (1, SEQ, D_MODEL)
    h = x.reshape(SEQ, D_MODEL)

    hn = _rmsnorm(h, gdn_norm[0], BF16)
    w_in_t = jnp.swapaxes(gdn_w_in, 1, 2)
    proj = _proj(hn, w_in_t, 0, GDN_MAIN_DIM, tm=2048, tn=1024, transposed=True)
    w_ba_t = w_in_t[0, GDN_MAIN_DIM:]
    g_row, b_row = _gdn_gates(hn, w_ba_t, gdn_a_log[0], gdn_dt_bias[0])
    mixed = _gdn_mixer(proj, gdn_conv[0], g_row, g_row.T, b_row.T, gdn_head_norm[0])
    h = _proj_residual(mixed, gdn_w_out, 0, h, tm=512, tn=1024, single_buffer=True)
    h = _ffn(h, 0, ffn_norm, ffn_w_gate, ffn_w_up, ffn_w_down)

    hn = _rmsnorm(h, hgrn_norm[0], BF16)
    proj = _proj(hn, hgrn_w_in, 0, 4 * HGRN_DIM, tm=2048, tn=1024)
    mixed = _hgrn_mixer(proj, hgrn_lower_bounds, hgrn_head_norm[0])
    h, hn = _proj_residual_norm(mixed, hgrn_w_out, 0, h, ffn_norm[1], tm=512)
    h = _ffn(h, 1, ffn_norm, ffn_w_gate, ffn_w_up, ffn_w_down, hn=hn)

    return _rmsnorm(h, final_norm, F32).reshape(1, SEQ, D_MODEL)
```

```python
import jax
import jax.numpy as jnp
from jax import lax
from jax.experimental import pallas as pl
from jax.experimental.pallas import tpu as pltpu

F32 = jnp.float32
BF16 = jnp.bfloat16

D_MODEL = 2048
SEQ = 8192
CHUNK = 64
HEAD_DIM = 128
EPS = 1e-6
LOG2E = 1.4426950408889634

GDN_K_HEADS = 16
GDN_V_HEADS = 32
GDN_KEY_DIM = GDN_K_HEADS * HEAD_DIM
GDN_VAL_DIM = GDN_V_HEADS * HEAD_DIM
GDN_CONV_DIM = 2 * GDN_KEY_DIM + GDN_VAL_DIM
GDN_MAIN_DIM = GDN_CONV_DIM + GDN_VAL_DIM
GDN_CONV_TAPS = 4

HGRN_HEADS = 16
HGRN_DIM = HGRN_HEADS * HEAD_DIM

VMEM_LIMIT_BYTES = 56 * 1024 * 1024
CARRY_ROWS = 8

GROUP = 2 * CHUNK
GDN_KH_PER_STEP = 16
GDN_ROWS = 128
HGRN_ROWS = 2048
HGRN_BIG_HALVES = (32, 16, 8)
HGRN_SMALL_HALVES = (4, 2)
HGRN_DIAG = 2


def _params(*semantics):
    return pltpu.CompilerParams(dimension_semantics=semantics, vmem_limit_bytes=VMEM_LIMIT_BYTES)


def _mm(a, b):
    return jnp.dot(a.astype(BF16), b.astype(BF16), preferred_element_type=F32)


def _mm_nt(a, b):
    return lax.dot_general(a.astype(BF16), b.astype(BF16), (((1,), (1,)), ((), ())),
                           preferred_element_type=F32)


def _mm_tn(a, b):
    return lax.dot_general(a.astype(BF16), b.astype(BF16), (((0,), (0,)), ((), ())),
                           preferred_element_type=F32)


def _sigmoid(x):
    return 1.0 / (1.0 + jnp.exp(-x))


def _silu(x):
    return x * _sigmoid(x)


def _rmsnorm_kernel(x_ref, w_ref, o_ref):
    x = x_ref[...]
    y = x * lax.rsqrt(jnp.mean(x * x, axis=-1, keepdims=True) + EPS)
    o_ref[...] = (y * w_ref[...]).astype(o_ref.dtype)


def _rmsnorm(x, w, out_dtype, rows=1024):
    m, d = x.shape
    return pl.pallas_call(
        _rmsnorm_kernel,
        grid=(m // rows,),
        in_specs=[pl.BlockSpec((rows, d), lambda i: (i, 0)),
                  pl.BlockSpec((1, d), lambda i: (0, 0))],
        out_specs=pl.BlockSpec((rows, d), lambda i: (i, 0)),
        out_shape=jax.ShapeDtypeStruct((m, d), out_dtype),
        compiler_params=_params("arbitrary"),
        name="rmsnorm",
    )(x, w.reshape(1, d))


def _weight_spec(k, tn, layer, first_tile=0, transposed=False):
    if transposed:
        return pl.BlockSpec((None, tn, k), lambda j, i: (layer, first_tile + j, 0))
    return pl.BlockSpec((None, k, tn), lambda j, i: (layer, 0, first_tile + j))


def _load_weight(w_ref, wb_ref):
    w = w_ref[...]
    if w.shape != wb_ref.shape:
        w = w.T
    wb_ref[...] = w.astype(BF16)


def _proj_kernel(x_ref, w_ref, o_ref, wb_ref):
    @pl.when(pl.program_id(1) == 0)
    def _():
        _load_weight(w_ref, wb_ref)

    o_ref[...] = jnp.dot(x_ref[...], wb_ref[...], preferred_element_type=F32).astype(o_ref.dtype)


def _proj(x, w, layer, n_cols, tm, tn, first_col=0, transposed=False):
    m, k = x.shape
    return pl.pallas_call(
        _proj_kernel,
        grid=(n_cols // tn, m // tm),
        in_specs=[pl.BlockSpec((tm, k), lambda j, i: (i, 0)),
                  _weight_spec(k, tn, layer, first_col // tn, transposed)],
        out_specs=pl.BlockSpec((tm, tn), lambda j, i: (i, j)),
        out_shape=jax.ShapeDtypeStruct((m, n_cols), BF16),
        scratch_shapes=[pltpu.VMEM((k, tn), BF16)],
        compiler_params=_params("arbitrary", "arbitrary"),
        name="proj",
    )(x, w)


def _proj_residual_kernel(x_ref, w_ref, r_ref, o_ref, wb_ref):
    @pl.when(pl.program_id(1) == 0)
    def _():
        wb_ref[...] = w_ref[...].astype(BF16)

    o_ref[...] = r_ref[...] + jnp.dot(x_ref[...], wb_ref[...], preferred_element_type=F32)


def _proj_residual(x, w, layer, res, tm, tn):
    m, k = x.shape
    n = w.shape[2]
    return pl.pallas_call(
        _proj_residual_kernel,
        grid=(n // tn, m // tm),
        in_specs=[pl.BlockSpec((tm, k), lambda j, i: (i, 0)), _weight_spec(k, tn, layer),
                  pl.BlockSpec((tm, tn), lambda j, i: (i, j))],
        out_specs=pl.BlockSpec((tm, tn), lambda j, i: (i, j)),
        out_shape=jax.ShapeDtypeStruct((m, n), F32),
        scratch_shapes=[pltpu.VMEM((k, tn), BF16)],
        compiler_params=_params("arbitrary", "arbitrary"),
        name="proj_residual",
    )(x, w, res)


def _proj_residual_norm_kernel(x_ref, w_ref, r_ref, nw_ref, o_ref, on_ref, wb_ref):
    @pl.when(pl.program_id(0) == 0)
    def _():
        wb_ref[...] = w_ref[...].astype(BF16)

    h = r_ref[...] + jnp.dot(x_ref[...], wb_ref[...], preferred_element_type=F32)
    o_ref[...] = h
    y = h * lax.rsqrt(jnp.mean(h * h, axis=-1, keepdims=True) + EPS)
    on_ref[...] = (y * nw_ref[...]).astype(on_ref.dtype)


def _proj_residual_norm(x, w, layer, res, norm_w, tm):
    m, k = x.shape
    n = w.shape[2]
    return pl.pallas_call(
        _proj_residual_norm_kernel,
        grid=(m // tm,),
        in_specs=[pl.BlockSpec((tm, k), lambda i: (i, 0)),
                  pl.BlockSpec((None, k, n), lambda i: (layer, 0, 0), pipeline_mode=pl.Buffered(1)),
                  pl.BlockSpec((tm, n), lambda i: (i, 0)),
                  pl.BlockSpec((1, n), lambda i: (0, 0))],
        out_specs=[pl.BlockSpec((tm, n), lambda i: (i, 0)), pl.BlockSpec((tm, n), lambda i: (i, 0))],
        out_shape=[jax.ShapeDtypeStruct((m, n), F32), jax.ShapeDtypeStruct((m, n), BF16)],
        scratch_shapes=[pltpu.VMEM((k, n), BF16)],
        compiler_params=_params("arbitrary"),
        name="proj_residual_norm",
    )(x, w, res, norm_w.reshape(1, n))


def _swiglu_up_kernel(x_ref, wg_ref, wu_ref, o_ref, wgb_ref, wub_ref):
    @pl.when(pl.program_id(1) == 0)
    def _():
        wgb_ref[...] = wg_ref[...].astype(BF16)
        wub_ref[...] = wu_ref[...].astype(BF16)

    x = x_ref[...]
    g = jnp.dot(x, wgb_ref[...], preferred_element_type=F32)
    u = jnp.dot(x, wub_ref[...], preferred_element_type=F32)
    o_ref[...] = (_silu(g) * u).astype(o_ref.dtype)


def _swiglu_up(x, w_gate, w_up, layer, tm, tn):
    m, k = x.shape
    n = w_gate.shape[2]
    return pl.pallas_call(
        _swiglu_up_kernel,
        grid=(n // tn, m // tm),
        in_specs=[pl.BlockSpec((tm, k), lambda j, i: (i, 0)), _weight_spec(k, tn, layer),
                  _weight_spec(k, tn, layer)],
        out_specs=pl.BlockSpec((tm, tn), lambda j, i: (i, j)),
        out_shape=jax.ShapeDtypeStruct((m, n), BF16),
        scratch_shapes=[pltpu.VMEM((k, tn), BF16), pltpu.VMEM((k, tn), BF16)],
        compiler_params=_params("arbitrary", "arbitrary"),
        name="swiglu_up",
    )(x, w_gate, w_up)


def _gdn_gate_kernel(x_ref, w_ref, alog_ref, dtb_ref, g_ref, beta_ref):
    ba = _mm_nt(w_ref[...], x_ref[...])
    nh = GDN_V_HEADS
    beta_ref[...] = _sigmoid(ba[:nh])
    a = ba[nh:] + dtb_ref[...]
    softplus = jnp.maximum(a, 0.0) + jnp.log1p(jnp.exp(-jnp.abs(a)))
    g = (-LOG2E * jnp.exp(alog_ref[...])) * softplus
    pos = lax.broadcasted_iota(jnp.int32, g.shape, 1) % CHUNK
    shift = 1
    while shift < CHUNK:
        g = g + jnp.where(pos >= shift, pltpu.roll(g, shift, 1), 0.0)
        shift *= 2
    g_ref[...] = g


def _gdn_gates(x, w_ba_t, a_log, dt_bias, rows=512):
    m, k = x.shape
    nh = GDN_V_HEADS
    return pl.pallas_call(
        _gdn_gate_kernel,
        grid=(m // rows,),
        in_specs=[pl.BlockSpec((rows, k), lambda i: (i, 0)),
                  pl.BlockSpec((2 * nh, k), lambda i: (0, 0)),
                  pl.BlockSpec((nh, 1), lambda i: (0, 0)),
                  pl.BlockSpec((nh, 1), lambda i: (0, 0))],
        out_specs=[pl.BlockSpec((nh, rows), lambda i: (0, i)),
                   pl.BlockSpec((nh, rows), lambda i: (0, i))],
        out_shape=[jax.ShapeDtypeStruct((nh, m), F32), jax.ShapeDtypeStruct((nh, m), F32)],
        compiler_params=_params("arbitrary"),
        name="gdn_gates",
    )(x, w_ba_t, a_log.reshape(nh, 1), dt_bias.reshape(nh, 1))


def _conv_silu(x_ref, carry_ref, w_ref):
    x = x_ref[...].astype(F32)
    rows, width = x.shape
    xe = jnp.concatenate([carry_ref[...], x], axis=0).reshape(rows // 8 + 1, 8, width)
    in_tile = lax.broadcasted_iota(jnp.int32, (1, 8, width), 1)
    w = w_ref[...]
    taps = GDN_CONV_TAPS
    y = w[taps - 1:taps] * x
    for j in range(1, taps):
        r = pltpu.roll(xe, j, 1)
        shifted = jnp.where(in_tile < j, r[:-1], r[1:]).reshape(rows, width)
        y = y + w[taps - 1 - j:taps - j] * shifted
    carry_ref[...] = x[rows - CARRY_ROWS:]
    return _silu(y)


def _l2norm(x):
    return x * lax.rsqrt(jnp.sum(x * x, axis=-1, keepdims=True) + 1e-6)


def _gated_rmsnorm(o, w, z):
    y = o * lax.rsqrt(jnp.mean(o * o, axis=-1, keepdims=True) + EPS)
    return y * w * _silu(z)


def _unit_lower_inverses(a_list, row, col):
    eye = (row == col).astype(F32)
    same16 = (row // 16) == (col // 16)
    same32 = (row // 32) == (col // 32)
    off32 = same32 & jnp.logical_not(same16)
    off64 = jnp.logical_not(same32)
    ad = [jnp.where(same16, a, 0.0) for a in a_list]
    n = row.shape[0]
    p = [_mm(x, x) for x in ad]
    t = [eye - x for x in ad]
    for _ in range(2):
        both = [_mm(jnp.concatenate([x, y], axis=0), y) for x, y in zip(t, p)]
        t = [x + b[:n] for x, b in zip(t, both)]
        p = [b[n:] for b in both]
    t = [x + _mm(x, y) for x, y in zip(t, p)]
    for mask in (off32, off64):
        y = [_mm(jnp.where(mask, a, 0.0), x) for a, x in zip(a_list, t)]
        t = [x - _mm(x, yy) for x, yy in zip(t, y)]
    return t


def _gdn_kernel(q_ref, k_ref, v_ref, z_ref, wq_ref, wk_ref, wv_ref, grow_ref,
                gcol_ref, bcol_ref, hn_ref, o_ref, s_ref, cq_ref, ck_ref, cv_ref):
    hb = pl.program_id(0)
    rows = q_ref.shape[0]
    n_groups = rows // GROUP
    n_chunks = rows // CHUNK
    per_group = GROUP // CHUNK

    @pl.when(pl.program_id(1) == 0)
    def _():
        s_ref[...] = jnp.zeros_like(s_ref)
        cq_ref[...] = jnp.zeros_like(cq_ref)
        ck_ref[...] = jnp.zeros_like(ck_ref)
        cv_ref[...] = jnp.zeros_like(cv_ref)

    q_all = _conv_silu(q_ref, cq_ref, wq_ref)
    k_all = _conv_silu(k_ref, ck_ref, wk_ref)
    v_all = _conv_silu(v_ref, cv_ref, wv_ref)
    hn = hn_ref[...]

    def lanes(x, i):
        return x[:, i * HEAD_DIM:(i + 1) * HEAD_DIM]

    qs = [_l2norm(lanes(q_all, j)) * (HEAD_DIM ** -0.5) for j in range(GDN_KH_PER_STEP)]
    ks = [_l2norm(lanes(k_all, j)) for j in range(GDN_KH_PER_STEP)]

    row = lax.broadcasted_iota(jnp.int32, (GROUP, GROUP), 0)
    col = lax.broadcasted_iota(jnp.int32, (GROUP, GROUP), 1)
    same_chunk = (row // CHUNK) == (col // CHUNK)
    causal = (row >= col) & same_chunk
    strict = row > col

    heads = [(j, e) for j in range(GDN_KH_PER_STEP) for e in range(2)]
    head_lane = lax.broadcasted_iota(jnp.int32, gcol_ref.shape, 1)
    gcol_all = gcol_ref[...]
    bcol_all = bcol_ref[...]
    g_col, b_col, g_row = [], [], []
    for j, e in heads:
        head = 2 * (GDN_KH_PER_STEP * hb + j) + e
        sel = head_lane == head
        g_col.append(jnp.sum(jnp.where(sel, gcol_all, 0.0), axis=1, keepdims=True))
        b_col.append(jnp.sum(jnp.where(sel, bcol_all, 0.0), axis=1, keepdims=True))
        g_row.append(grow_ref[pl.ds(head, 1), :])

    def grp(x, g):
        return x[g * GROUP:(g + 1) * GROUP]

    kq = [[_mm_nt(jnp.concatenate([grp(qs[j], g), grp(ks[j], g)], axis=0), grp(ks[j], g))
           for g in range(n_groups)] for j in range(GDN_KH_PER_STEP)]

    items = [(h, g) for h in range(len(heads)) for g in range(n_groups)]
    a_list, aqk_list, rhs_list, qd_list, kd_list = [], [], [], [], []
    for h, g in items:
        j, e = heads[h]
        gc = grp(g_col[h], g)
        bc = grp(b_col[h], g)
        gr = g_row[h][:, g * GROUP:(g + 1) * GROUP]
        decay = jnp.exp2(jnp.where(causal, gc - gr, -jnp.inf))
        a_list.append(jnp.where(strict, bc * kq[j][g][GROUP:] * decay, 0.0))
        aqk_list.append(kq[j][g][:GROUP] * decay)
        eg = jnp.exp2(gc)
        kg = grp(ks[j], g)
        vg = grp(lanes(v_all, 2 * j + e), g)
        rhs_list.append(jnp.concatenate([vg * bc, kg * (bc * eg)], axis=1))
        g_last = jnp.concatenate(
            [jnp.broadcast_to(gc[(i + 1) * CHUNK - 1:(i + 1) * CHUNK], (CHUNK, 1)) for i in range(per_group)],
            axis=0)
        qd_list.append(grp(qs[j], g) * eg)
        kd_list.append(kg * jnp.exp2(g_last - gc))

    t_list = _unit_lower_inverses(a_list, row, col)
    uw_list = [_mm(t, r) for t, r in zip(t_list, rhs_list)]

    states = [s_ref[h] for h in range(len(heads))]
    for c in range(n_chunks):
        g, i = divmod(c, per_group)
        sl = slice(i * CHUNK, (i + 1) * CHUNK)
        idx = [h * n_groups + g for h in range(len(heads))]
        ws_qs = [_mm(jnp.concatenate([uw_list[n][sl, HEAD_DIM:], qd_list[n][sl]], axis=0), states[h])
                 for h, n in enumerate(idx)]
        v_new = [uw_list[n][sl, :HEAD_DIM] - r[:CHUNK] for n, r in zip(idx, ws_qs)]
        intra = [_mm(aqk_list[n][sl, i * CHUNK:(i + 1) * CHUNK], vn) for n, vn in zip(idx, v_new)]
        upd = [_mm_tn(kd_list[n][sl], vn) for n, vn in zip(idx, v_new)]
        for h in range(len(heads)):
            j, e = heads[h]
            g_last = g_col[h][(c + 1) * CHUNK - 1:(c + 1) * CHUNK]
            states[h] = states[h] * jnp.exp2(g_last) + upd[h]
            o = ws_qs[h][CHUNK:] + intra[h]
            z = z_ref[c * CHUNK:(c + 1) * CHUNK, (2 * j + e) * HEAD_DIM:(2 * j + e + 1) * HEAD_DIM].astype(F32)
            o_ref[c * CHUNK:(c + 1) * CHUNK, (2 * j + e) * HEAD_DIM:(2 * j + e + 1) * HEAD_DIM] = (
                _gated_rmsnorm(o, hn, z).astype(o_ref.dtype))
    for h in range(len(heads)):
        s_ref[h] = states[h]


def _gdn_mixer(proj, conv_w, g_row, g_col, b_col, head_norm):
    m = proj.shape[0]
    rows = GDN_ROWS
    nh = GDN_V_HEADS
    kw = GDN_KH_PER_STEP * HEAD_DIM
    vw = 2 * kw
    kb = GDN_KEY_DIM // kw
    vb = 2 * GDN_KEY_DIM // vw
    zb = GDN_CONV_DIM // vw
    return pl.pallas_call(
        _gdn_kernel,
        grid=(GDN_K_HEADS // GDN_KH_PER_STEP, m // rows),
        in_specs=[
            pl.BlockSpec((rows, kw), lambda h, t: (t, h)),
            pl.BlockSpec((rows, kw), lambda h, t: (t, kb + h)),
            pl.BlockSpec((rows, vw), lambda h, t: (t, vb + h)),
            pl.BlockSpec((rows, vw), lambda h, t: (t, zb + h)),
            pl.BlockSpec((GDN_CONV_TAPS, kw), lambda h, t: (0, h)),
            pl.BlockSpec((GDN_CONV_TAPS, kw), lambda h, t: (0, kb + h)),
            pl.BlockSpec((GDN_CONV_TAPS, vw), lambda h, t: (0, vb + h)),
            pl.BlockSpec((nh, rows), lambda h, t: (0, t)),
            pl.BlockSpec((rows, nh), lambda h, t: (t, 0)),
            pl.BlockSpec((rows, nh), lambda h, t: (t, 0)),
            pl.BlockSpec((1, HEAD_DIM), lambda h, t: (0, 0)),
        ],
        out_specs=pl.BlockSpec((rows, vw), lambda h, t: (t, h)),
        out_shape=jax.ShapeDtypeStruct((m, GDN_VAL_DIM), BF16),
        scratch_shapes=[
            pltpu.VMEM((2 * GDN_KH_PER_STEP, HEAD_DIM, HEAD_DIM), F32),
            pltpu.VMEM((CARRY_ROWS, kw), F32),
            pltpu.VMEM((CARRY_ROWS, kw), F32),
            pltpu.VMEM((CARRY_ROWS, vw), F32),
        ],
        compiler_params=_params("arbitrary", "arbitrary"),
        name="gdn_mixer",
    )(proj, proj, proj, proj, conv_w, conv_w, conv_w, g_row, g_col, b_col,
      head_norm.reshape(1, HEAD_DIM))


def _row_broadcast(x, first, period, rows):
    parts = [jnp.broadcast_to(x[r:r + 1], (period, x.shape[1])) for r in range(first, rows, period)]
    return jnp.concatenate(parts, axis=0)


def _hgrn_kernel(q_ref, f_ref, i_ref, g_ref, lb_ref, hn_ref, o_ref, s_ref):
    rows = q_ref.shape[0]
    n_groups = rows // GROUP
    n_chunks = rows // CHUNK
    per_group = GROUP // CHUNK

    @pl.when(pl.program_id(1) == 0)
    def _():
        s_ref[...] = jnp.zeros_like(s_ref)

    lbs = lb_ref[...]
    mx = jnp.max(lbs, axis=0, keepdims=True)
    ex = jnp.exp(lbs - mx)
    sm = ex / jnp.sum(ex, axis=0, keepdims=True)
    lb = (sm[0:1] + sm[1:2]) - sm[0:1]

    q = _silu(q_ref[...].astype(F32))
    fl = f_ref[...].astype(F32)
    v = i_ref[...].astype(F32)
    en = jnp.exp(-jnp.abs(fl))
    rc = 1.0 / (1.0 + en)
    pos = fl >= 0.0
    sig = jnp.where(pos, rc, en * rc)
    nsig = jnp.where(pos, en * rc, rc)
    logf = jnp.log(lb + (1.0 - lb) * sig)
    kk = (1.0 - lb) * nsig

    rowi = lax.broadcasted_iota(jnp.int32, (rows, HEAD_DIM), 0)
    row = lax.broadcasted_iota(jnp.int32, (GROUP, GROUP), 0)
    col = lax.broadcasted_iota(jnp.int32, (GROUP, GROUP), 1)

    tri = (((row // CHUNK) == (col // CHUNK)) & (row >= col)).astype(BF16)
    logf2 = logf * LOG2E
    hi = logf2.astype(BF16)
    rest = logf2 - hi.astype(F32)
    mid = rest.astype(BF16)
    lo = (rest - mid.astype(F32)).astype(BF16)
    pieces = jnp.concatenate([hi, mid, lo], axis=1)

    def mid_range(s):
        mid_row = (row // (2 * s)) * (2 * s) + (s - 1)
        return ((col > mid_row) & (col <= row)) | ((col > row) & (col <= mid_row))

    small_ranges = jnp.concatenate([mid_range(s).astype(BF16) for s in HGRN_SMALL_HALVES], axis=0)
    b_parts, small_parts = [], []
    for g in range(n_groups):
        pg = pieces[g * GROUP:(g + 1) * GROUP]
        s3 = jnp.dot(tri, pg, preferred_element_type=F32)
        b_parts.append(s3[:, :HEAD_DIM] + s3[:, HEAD_DIM:2 * HEAD_DIM] + s3[:, 2 * HEAD_DIM:])
        s2 = jnp.dot(small_ranges, pg[:, :2 * HEAD_DIM], preferred_element_type=F32)
        small_parts.append(s2[:, :HEAD_DIM] + s2[:, HEAD_DIM:])
    b = jnp.concatenate(b_parts, axis=0)
    b_last = _row_broadcast(b, CHUNK - 1, CHUNK, rows)
    qd = q * jnp.exp2(b)
    kd = kk * jnp.exp2(b_last - b)

    halves = HGRN_BIG_HALVES + HGRN_SMALL_HALVES
    level_x = []
    for s in HGRN_BIG_HALVES:
        b_mid = _row_broadcast(b, s - 1, 2 * s, rows)
        lower = (rowi % (2 * s)) >= s
        level_x.append(jnp.where(lower, q, kk) * jnp.exp2(-jnp.abs(b - b_mid)))
    for n, s in enumerate(HGRN_SMALL_HALVES):
        expo = jnp.concatenate([p[n * GROUP:(n + 1) * GROUP] for p in small_parts], axis=0)
        lower = (rowi % (2 * s)) >= s
        level_x.append(jnp.where(lower, q, kk) * jnp.exp2(expo))

    def rot8(x, d):
        return pltpu.roll(x.reshape(rows // 8, 8, HEAD_DIM), d, 1).reshape(rows, HEAD_DIM)

    diags = [jnp.sum(q * kk, axis=1, keepdims=True)]
    for d in range(1, HGRN_DIAG):
        e = jnp.exp2(jnp.minimum(b - rot8(b, d), 0.0))
        diags.append(jnp.sum(q * rot8(kk, d) * e, axis=1, keepdims=True))

    def grp(x, g):
        return x[g * GROUP:(g + 1) * GROUP]

    def chk(x, c):
        return x[c * CHUNK:(c + 1) * CHUNK]

    level_mm = [[_mm_nt(grp(x, g), grp(x, g)) for x in level_x] for g in range(n_groups)]
    kv = [_mm_tn(chk(v, c), chk(kd, c)) for c in range(n_chunks)]

    scores = []
    for g in range(n_groups):
        sc = jnp.zeros((GROUP, GROUP), F32)
        for s, mm in zip(halves, level_mm[g]):
            pair = ((row // (2 * s)) == (col // (2 * s))) & ((row % (2 * s)) >= s) & ((col % (2 * s)) < s)
            sc = jnp.where(pair, mm, sc)
        same_diag = (row // HGRN_DIAG) == (col // HGRN_DIAG)
        for d in range(HGRN_DIAG):
            sc = jnp.where(same_diag & (row - col == d), grp(diags[d], g), sc)
        scores.append(sc)

    state_t = s_ref[...]
    states = []
    for c in range(n_chunks):
        states.append(state_t)
        f_last = jnp.exp2(b[(c + 1) * CHUNK - 1:(c + 1) * CHUNK])
        state_t = state_t * f_last + kv[c]
    s_ref[...] = state_t

    intra = [_mm(scores[g], grp(v, g)) for g in range(n_groups)]
    inter = [_mm_nt(chk(qd, c), states[c]) for c in range(n_chunks)]
    hn = hn_ref[...]
    for c in range(n_chunks):
        g, i = divmod(c, per_group)
        o = inter[c] + intra[g][i * CHUNK:(i + 1) * CHUNK]
        gate = g_ref[c * CHUNK:(c + 1) * CHUNK, :].astype(F32)
        o_ref[c * CHUNK:(c + 1) * CHUNK, :] = _gated_rmsnorm(o, hn, gate).astype(o_ref.dtype)


def _hgrn_mixer(proj, lower_bounds, head_norm):
    m = proj.shape[0]
    rows = HGRN_ROWS
    nb = HGRN_DIM // HEAD_DIM
    return pl.pallas_call(
        _hgrn_kernel,
        grid=(HGRN_HEADS, m // rows),
        in_specs=[
            pl.BlockSpec((rows, HEAD_DIM), lambda h, t: (t, h)),
            pl.BlockSpec((rows, HEAD_DIM), lambda h, t: (t, nb + h)),
            pl.BlockSpec((rows, HEAD_DIM), lambda h, t: (t, 2 * nb + h)),
            pl.BlockSpec((rows, HEAD_DIM), lambda h, t: (t, 3 * nb + h)),
            pl.BlockSpec((2, HEAD_DIM), lambda h, t: (0, h)),
            pl.BlockSpec((1, HEAD_DIM), lambda h, t: (0, 0)),
        ],
        out_specs=pl.BlockSpec((rows, HEAD_DIM), lambda h, t: (t, h)),
        out_shape=jax.ShapeDtypeStruct((m, HGRN_DIM), BF16),
        scratch_shapes=[pltpu.VMEM((HEAD_DIM, HEAD_DIM), F32)],
        compiler_params=_params("arbitrary", "arbitrary"),
        name="hgrn_mixer",
    )(proj, proj, proj, proj, lower_bounds, head_norm.reshape(1, HEAD_DIM))


def _ffn(h, layer, norm_w, w_gate, w_up, w_down, hn=None):
    if hn is None:
        hn = _rmsnorm(h, norm_w[layer], BF16)
    act = _swiglu_up(hn, w_gate, w_up, layer, tm=1024, tn=512)
    return _proj_residual(act, w_down, layer, h, tm=512, tn=512)


def kernel(x, gdn_norm, gdn_w_in, gdn_conv, gdn_a_log, gdn_dt_bias, gdn_head_norm, gdn_w_out,
           hgrn_norm, hgrn_w_in, hgrn_lower_bounds, hgrn_head_norm, hgrn_w_out,
           ffn_norm, ffn_w_gate, ffn_w_up, ffn_w_down, final_norm):
    assert x.shape == (1, SEQ, D_MODEL)
    h = x.reshape(SEQ, D_MODEL)

    hn = _rmsnorm(h, gdn_norm[0], BF16)
    w_in_t = jnp.swapaxes(gdn_w_in, 1, 2)
    proj = _proj(hn, w_in_t, 0, GDN_MAIN_DIM, tm=2048, tn=1024, transposed=True)
    w_ba_t = w_in_t[0, GDN_MAIN_DIM:]
    g_row, b_row = _gdn_gates(hn, w_ba_t, gdn_a_log[0], gdn_dt_bias[0])
    mixed = _gdn_mixer(proj, gdn_conv[0], g_row, g_row.T, b_row.T, gdn_head_norm[0])
    h = _proj_residual(mixed, gdn_w_out, 0, h, tm=1024, tn=512)
    h = _ffn(h, 0, ffn_norm, ffn_w_gate, ffn_w_up, ffn_w_down)

    hn = _rmsnorm(h, hgrn_norm[0], BF16)
    proj = _proj(hn, hgrn_w_in, 0, 4 * HGRN_DIM, tm=2048, tn=1024)
    mixed = _hgrn_mixer(proj, hgrn_lower_bounds, hgrn_head_norm[0])
    h, hn = _proj_residual_norm(mixed, hgrn_w_out, 0, h, ffn_norm[1], tm=512)
    h = _ffn(h, 1, ffn_norm, ffn_w_gate, ffn_w_up, ffn_w_down, hn=hn)

    return _rmsnorm(h, final_norm, F32).reshape(1, SEQ, D_MODEL)
```

```python
import jax
import jax.numpy as jnp
from jax import lax
from jax.experimental import pallas as pl
from jax.experimental.pallas import tpu as pltpu

F32 = jnp.float32
BF16 = jnp.bfloat16

D_MODEL = 2048
SEQ = 8192
CHUNK = 64
HEAD_DIM = 128
EPS = 1e-6
LOG2E = 1.4426950408889634

GDN_K_HEADS = 16
GDN_V_HEADS = 32
GDN_KEY_DIM = GDN_K_HEADS * HEAD_DIM
GDN_VAL_DIM = GDN_V_HEADS * HEAD_DIM
GDN_CONV_DIM = 2 * GDN_KEY_DIM + GDN_VAL_DIM
GDN_MAIN_DIM = GDN_CONV_DIM + GDN_VAL_DIM
GDN_CONV_TAPS = 4

HGRN_HEADS = 16
HGRN_DIM = HGRN_HEADS * HEAD_DIM

VMEM_LIMIT_BYTES = 56 * 1024 * 1024
CARRY_ROWS = 8

GROUP = 2 * CHUNK
GDN_KH_PER_STEP = 16
GDN_ROWS = 128
HGRN_ROWS = 4096
HGRN_BIG_HALVES = (32, 16, 8)
HGRN_SMALL_HALVES = (4, 2)
HGRN_DIAG = 2


def _params(*semantics):
    return pltpu.CompilerParams(dimension_semantics=semantics, vmem_limit_bytes=VMEM_LIMIT_BYTES)


def _mm(a, b):
    return jnp.dot(a.astype(BF16), b.astype(BF16), preferred_element_type=F32)


def _mm_nt(a, b):
    return lax.dot_general(a.astype(BF16), b.astype(BF16), (((1,), (1,)), ((), ())),
                           preferred_element_type=F32)


def _mm_tn(a, b):
    return lax.dot_general(a.astype(BF16), b.astype(BF16), (((0,), (0,)), ((), ())),
                           preferred_element_type=F32)


def _sigmoid(x):
    return 1.0 / (1.0 + jnp.exp(-x))


def _silu(x):
    h = 0.5 * x
    return h + h * jnp.tanh(h)


def _rmsnorm_kernel(x_ref, w_ref, o_ref):
    x = x_ref[...]
    y = x * lax.rsqrt(jnp.mean(x * x, axis=-1, keepdims=True) + EPS)
    o_ref[...] = (y * w_ref[...]).astype(o_ref.dtype)


def _rmsnorm(x, w, out_dtype, rows=1024):
    m, d = x.shape
    return pl.pallas_call(
        _rmsnorm_kernel,
        grid=(m // rows,),
        in_specs=[pl.BlockSpec((rows, d), lambda i: (i, 0)),
                  pl.BlockSpec((1, d), lambda i: (0, 0))],
        out_specs=pl.BlockSpec((rows, d), lambda i: (i, 0)),
        out_shape=jax.ShapeDtypeStruct((m, d), out_dtype),
        compiler_params=_params("arbitrary"),
        name="rmsnorm",
    )(x, w.reshape(1, d))


def _weight_spec(k, tn, layer, first_tile=0, transposed=False):
    if transposed:
        return pl.BlockSpec((None, tn, k), lambda j, i: (layer, first_tile + j, 0))
    return pl.BlockSpec((None, k, tn), lambda j, i: (layer, 0, first_tile + j))


def _load_weight(w_ref, wb_ref):
    w = w_ref[...]
    if w.shape != wb_ref.shape:
        w = w.T
    wb_ref[...] = w.astype(BF16)


def _proj_kernel(x_ref, w_ref, o_ref, wb_ref):
    @pl.when(pl.program_id(1) == 0)
    def _():
        _load_weight(w_ref, wb_ref)

    o_ref[...] = jnp.dot(x_ref[...], wb_ref[...], preferred_element_type=F32).astype(o_ref.dtype)


def _proj(x, w, layer, n_cols, tm, tn, first_col=0, transposed=False):
    m, k = x.shape
    return pl.pallas_call(
        _proj_kernel,
        grid=(n_cols // tn, m // tm),
        in_specs=[pl.BlockSpec((tm, k), lambda j, i: (i, 0)),
                  _weight_spec(k, tn, layer, first_col // tn, transposed)],
        out_specs=pl.BlockSpec((tm, tn), lambda j, i: (i, j)),
        out_shape=jax.ShapeDtypeStruct((m, n_cols), BF16),
        scratch_shapes=[pltpu.VMEM((k, tn), BF16)],
        compiler_params=_params("arbitrary", "arbitrary"),
        name="proj",
    )(x, w)


def _proj_residual_kernel(x_ref, w_ref, r_ref, o_ref, wb_ref):
    @pl.when(pl.program_id(1) == 0)
    def _():
        wb_ref[...] = w_ref[...].astype(BF16)

    o_ref[...] = r_ref[...] + jnp.dot(x_ref[...], wb_ref[...], preferred_element_type=F32)


def _proj_residual(x, w, layer, res, tm, tn):
    m, k = x.shape
    n = w.shape[2]
    return pl.pallas_call(
        _proj_residual_kernel,
        grid=(n // tn, m // tm),
        in_specs=[pl.BlockSpec((tm, k), lambda j, i: (i, 0)), _weight_spec(k, tn, layer),
                  pl.BlockSpec((tm, tn), lambda j, i: (i, j))],
        out_specs=pl.BlockSpec((tm, tn), lambda j, i: (i, j)),
        out_shape=jax.ShapeDtypeStruct((m, n), F32),
        scratch_shapes=[pltpu.VMEM((k, tn), BF16)],
        compiler_params=_params("arbitrary", "arbitrary"),
        name="proj_residual",
    )(x, w, res)


def _proj_residual_norm_kernel(x_ref, w_ref, r_ref, nw_ref, o_ref, on_ref, wb_ref):
    @pl.when(pl.program_id(0) == 0)
    def _():
        wb_ref[...] = w_ref[...].astype(BF16)

    h = r_ref[...] + jnp.dot(x_ref[...], wb_ref[...], preferred_element_type=F32)
    o_ref[...] = h
    y = h * lax.rsqrt(jnp.mean(h * h, axis=-1, keepdims=True) + EPS)
    on_ref[...] = (y * nw_ref[...]).astype(on_ref.dtype)


def _proj_residual_norm(x, w, layer, res, norm_w, tm):
    m, k = x.shape
    n = w.shape[2]
    return pl.pallas_call(
        _proj_residual_norm_kernel,
        grid=(m // tm,),
        in_specs=[pl.BlockSpec((tm, k), lambda i: (i, 0)),
                  pl.BlockSpec((None, k, n), lambda i: (layer, 0, 0), pipeline_mode=pl.Buffered(1)),
                  pl.BlockSpec((tm, n), lambda i: (i, 0)),
                  pl.BlockSpec((1, n), lambda i: (0, 0))],
        out_specs=[pl.BlockSpec((tm, n), lambda i: (i, 0)), pl.BlockSpec((tm, n), lambda i: (i, 0))],
        out_shape=[jax.ShapeDtypeStruct((m, n), F32), jax.ShapeDtypeStruct((m, n), BF16)],
        scratch_shapes=[pltpu.VMEM((k, n), BF16)],
        compiler_params=_params("arbitrary"),
        name="proj_residual_norm",
    )(x, w, res, norm_w.reshape(1, n))


def _swiglu_up_kernel(x_ref, wg_ref, wu_ref, o_ref, wgb_ref, wub_ref):
    @pl.when(pl.program_id(1) == 0)
    def _():
        wgb_ref[...] = wg_ref[...].astype(BF16)
        wub_ref[...] = wu_ref[...].astype(BF16)

    x = x_ref[...]
    g = jnp.dot(x, wgb_ref[...], preferred_element_type=F32)
    u = jnp.dot(x, wub_ref[...], preferred_element_type=F32)
    o_ref[...] = (_silu(g) * u).astype(o_ref.dtype)


def _swiglu_up(x, w_gate, w_up, layer, tm, tn):
    m, k = x.shape
    n = w_gate.shape[2]
    return pl.pallas_call(
        _swiglu_up_kernel,
        grid=(n // tn, m // tm),
        in_specs=[pl.BlockSpec((tm, k), lambda j, i: (i, 0)), _weight_spec(k, tn, layer),
                  _weight_spec(k, tn, layer)],
        out_specs=pl.BlockSpec((tm, tn), lambda j, i: (i, j)),
        out_shape=jax.ShapeDtypeStruct((m, n), BF16),
        scratch_shapes=[pltpu.VMEM((k, tn), BF16), pltpu.VMEM((k, tn), BF16)],
        compiler_params=_params("arbitrary", "arbitrary"),
        name="swiglu_up",
    )(x, w_gate, w_up)


def _gdn_gate_kernel(x_ref, w_ref, alog_ref, dtb_ref, g_ref, beta_ref):
    ba = _mm_nt(w_ref[...], x_ref[...])
    nh = GDN_V_HEADS
    beta_ref[...] = _sigmoid(ba[:nh])
    a = ba[nh:] + dtb_ref[...]
    softplus = jnp.maximum(a, 0.0) + jnp.log1p(jnp.exp(-jnp.abs(a)))
    g = (-LOG2E * jnp.exp(alog_ref[...])) * softplus
    pos = lax.broadcasted_iota(jnp.int32, g.shape, 1) % CHUNK
    shift = 1
    while shift < CHUNK:
        g = g + jnp.where(pos >= shift, pltpu.roll(g, shift, 1), 0.0)
        shift *= 2
    g_ref[...] = g


def _gdn_gates(x, w_ba_t, a_log, dt_bias, rows=512):
    m, k = x.shape
    nh = GDN_V_HEADS
    return pl.pallas_call(
        _gdn_gate_kernel,
        grid=(m // rows,),
        in_specs=[pl.BlockSpec((rows, k), lambda i: (i, 0)),
                  pl.BlockSpec((2 * nh, k), lambda i: (0, 0)),
                  pl.BlockSpec((nh, 1), lambda i: (0, 0)),
                  pl.BlockSpec((nh, 1), lambda i: (0, 0))],
        out_specs=[pl.BlockSpec((nh, rows), lambda i: (0, i)),
                   pl.BlockSpec((nh, rows), lambda i: (0, i))],
        out_shape=[jax.ShapeDtypeStruct((nh, m), F32), jax.ShapeDtypeStruct((nh, m), F32)],
        compiler_params=_params("arbitrary"),
        name="gdn_gates",
    )(x, w_ba_t, a_log.reshape(nh, 1), dt_bias.reshape(nh, 1))


def _conv_silu(x_ref, carry_ref, w_ref):
    x = x_ref[...].astype(F32)
    rows, width = x.shape
    xe = jnp.concatenate([carry_ref[...], x], axis=0).reshape(rows // 8 + 1, 8, width)
    in_tile = lax.broadcasted_iota(jnp.int32, (1, 8, width), 1)
    w = w_ref[...]
    taps = GDN_CONV_TAPS
    y = w[taps - 1:taps] * x
    for j in range(1, taps):
        r = pltpu.roll(xe, j, 1)
        shifted = jnp.where(in_tile < j, r[:-1], r[1:]).reshape(rows, width)
        y = y + w[taps - 1 - j:taps - j] * shifted
    carry_ref[...] = x[rows - CARRY_ROWS:]
    return _silu(y)


def _l2norm(x):
    return x * lax.rsqrt(jnp.sum(x * x, axis=-1, keepdims=True) + 1e-6)


def _gated_rmsnorm(o, w, z):
    y = o * lax.rsqrt(jnp.mean(o * o, axis=-1, keepdims=True) + EPS)
    return y * w * _silu(z)


def _unit_lower_inverses(a_list, row, col):
    eye = (row == col).astype(F32)
    same16 = (row // 16) == (col // 16)
    same32 = (row // 32) == (col // 32)
    off32 = same32 & jnp.logical_not(same16)
    off64 = jnp.logical_not(same32)
    ad = [jnp.where(same16, a, 0.0) for a in a_list]
    n = row.shape[0]
    p = [_mm(x, x) for x in ad]
    t = [eye - x for x in ad]
    for _ in range(2):
        both = [_mm(jnp.concatenate([x, y], axis=0), y) for x, y in zip(t, p)]
        t = [x + b[:n] for x, b in zip(t, both)]
        p = [b[n:] for b in both]
    t = [x + _mm(x, y) for x, y in zip(t, p)]
    for mask in (off32, off64):
        y = [_mm(jnp.where(mask, a, 0.0), x) for a, x in zip(a_list, t)]
        t = [x - _mm(x, yy) for x, yy in zip(t, y)]
    return t


def _gdn_kernel(q_ref, k_ref, v_ref, z_ref, wq_ref, wk_ref, wv_ref, grow_ref,
                gcol_ref, bcol_ref, hn_ref, o_ref, s_ref, cq_ref, ck_ref, cv_ref):
    hb = pl.program_id(0)
    rows = q_ref.shape[0]
    n_groups = rows // GROUP
    n_chunks = rows // CHUNK
    per_group = GROUP // CHUNK

    @pl.when(pl.program_id(1) == 0)
    def _():
        s_ref[...] = jnp.zeros_like(s_ref)
        cq_ref[...] = jnp.zeros_like(cq_ref)
        ck_ref[...] = jnp.zeros_like(ck_ref)
        cv_ref[...] = jnp.zeros_like(cv_ref)

    q_all = _conv_silu(q_ref, cq_ref, wq_ref)
    k_all = _conv_silu(k_ref, ck_ref, wk_ref)
    v_all = _conv_silu(v_ref, cv_ref, wv_ref)
    hn = hn_ref[...]

    def lanes(x, i):
        return x[:, i * HEAD_DIM:(i + 1) * HEAD_DIM]

    qs = [_l2norm(lanes(q_all, j)) * (HEAD_DIM ** -0.5) for j in range(GDN_KH_PER_STEP)]
    ks = [_l2norm(lanes(k_all, j)) for j in range(GDN_KH_PER_STEP)]

    row = lax.broadcasted_iota(jnp.int32, (GROUP, GROUP), 0)
    col = lax.broadcasted_iota(jnp.int32, (GROUP, GROUP), 1)
    same_chunk = (row // CHUNK) == (col // CHUNK)
    causal = (row >= col) & same_chunk
    strict = row > col

    heads = [(j, e) for j in range(GDN_KH_PER_STEP) for e in range(2)]
    head_lane = lax.broadcasted_iota(jnp.int32, gcol_ref.shape, 1)
    gcol_all = gcol_ref[...]
    bcol_all = bcol_ref[...]
    g_col, b_col, g_row = [], [], []
    for j, e in heads:
        head = 2 * (GDN_KH_PER_STEP * hb + j) + e
        sel = head_lane == head
        g_col.append(jnp.sum(jnp.where(sel, gcol_all, 0.0), axis=1, keepdims=True))
        b_col.append(jnp.sum(jnp.where(sel, bcol_all, 0.0), axis=1, keepdims=True))
        g_row.append(grow_ref[pl.ds(head, 1), :])

    def grp(x, g):
        return x[g * GROUP:(g + 1) * GROUP]

    kq = [[_mm_nt(jnp.concatenate([grp(qs[j], g), grp(ks[j], g)], axis=0), grp(ks[j], g))
           for g in range(n_groups)] for j in range(GDN_KH_PER_STEP)]

    items = [(h, g) for h in range(len(heads)) for g in range(n_groups)]
    a_list, aqk_list, rhs_list, qd_list, kd_list = [], [], [], [], []
    for h, g in items:
        j, e = heads[h]
        gc = grp(g_col[h], g)
        bc = grp(b_col[h], g)
        gr = g_row[h][:, g * GROUP:(g + 1) * GROUP]
        decay = jnp.exp2(jnp.where(causal, gc - gr, -jnp.inf))
        a_list.append(jnp.where(strict, bc * kq[j][g][GROUP:] * decay, 0.0))
        aqk_list.append(kq[j][g][:GROUP] * decay)
        eg = jnp.exp2(gc)
        kg = grp(ks[j], g)
        vg = grp(lanes(v_all, 2 * j + e), g)
        rhs_list.append(jnp.concatenate([vg * bc, kg * (bc * eg)], axis=1))
        g_last = jnp.concatenate(
            [jnp.broadcast_to(gc[(i + 1) * CHUNK - 1:(i + 1) * CHUNK], (CHUNK, 1)) for i in range(per_group)],
            axis=0)
        qd_list.append(grp(qs[j], g) * eg)
        kd_list.append(kg * jnp.exp2(g_last - gc))

    t_list = _unit_lower_inverses(a_list, row, col)
    uw_list = [_mm(t, r) for t, r in zip(t_list, rhs_list)]

    states = [s_ref[h] for h in range(len(heads))]
    for c in range(n_chunks):
        g, i = divmod(c, per_group)
        sl = slice(i * CHUNK, (i + 1) * CHUNK)
        idx = [h * n_groups + g for h in range(len(heads))]
        ws_qs = [_mm(jnp.concatenate([uw_list[n][sl, HEAD_DIM:], qd_list[n][sl]], axis=0), states[h])
                 for h, n in enumerate(idx)]
        v_new = [uw_list[n][sl, :HEAD_DIM] - r[:CHUNK] for n, r in zip(idx, ws_qs)]
        intra = [_mm(aqk_list[n][sl, i * CHUNK:(i + 1) * CHUNK], vn) for n, vn in zip(idx, v_new)]
        upd = [_mm_tn(kd_list[n][sl], vn) for n, vn in zip(idx, v_new)]
        for h in range(len(heads)):
            j, e = heads[h]
            g_last = g_col[h][(c + 1) * CHUNK - 1:(c + 1) * CHUNK]
            states[h] = states[h] * jnp.exp2(g_last) + upd[h]
            o = ws_qs[h][CHUNK:] + intra[h]
            z = z_ref[c * CHUNK:(c + 1) * CHUNK, (2 * j + e) * HEAD_DIM:(2 * j + e + 1) * HEAD_DIM].astype(F32)
            o_ref[c * CHUNK:(c + 1) * CHUNK, (2 * j + e) * HEAD_DIM:(2 * j + e + 1) * HEAD_DIM] = (
                _gated_rmsnorm(o, hn, z).astype(o_ref.dtype))
    for h in range(len(heads)):
        s_ref[h] = states[h]


def _gdn_mixer(proj, conv_w, g_row, g_col, b_col, head_norm):
    m = proj.shape[0]
    rows = GDN_ROWS
    nh = GDN_V_HEADS
    kw = GDN_KH_PER_STEP * HEAD_DIM
    vw = 2 * kw
    kb = GDN_KEY_DIM // kw
    vb = 2 * GDN_KEY_DIM // vw
    zb = GDN_CONV_DIM // vw
    return pl.pallas_call(
        _gdn_kernel,
        grid=(GDN_K_HEADS // GDN_KH_PER_STEP, m // rows),
        in_specs=[
            pl.BlockSpec((rows, kw), lambda h, t: (t, h)),
            pl.BlockSpec((rows, kw), lambda h, t: (t, kb + h)),
            pl.BlockSpec((rows, vw), lambda h, t: (t, vb + h)),
            pl.BlockSpec((rows, vw), lambda h, t: (t, zb + h)),
            pl.BlockSpec((GDN_CONV_TAPS, kw), lambda h, t: (0, h)),
            pl.BlockSpec((GDN_CONV_TAPS, kw), lambda h, t: (0, kb + h)),
            pl.BlockSpec((GDN_CONV_TAPS, vw), lambda h, t: (0, vb + h)),
            pl.BlockSpec((nh, rows), lambda h, t: (0, t)),
            pl.BlockSpec((rows, nh), lambda h, t: (t, 0)),
            pl.BlockSpec((rows, nh), lambda h, t: (t, 0)),
            pl.BlockSpec((1, HEAD_DIM), lambda h, t: (0, 0)),
        ],
        out_specs=pl.BlockSpec((rows, vw), lambda h, t: (t, h)),
        out_shape=jax.ShapeDtypeStruct((m, GDN_VAL_DIM), BF16),
        scratch_shapes=[
            pltpu.VMEM((2 * GDN_KH_PER_STEP, HEAD_DIM, HEAD_DIM), F32),
            pltpu.VMEM((CARRY_ROWS, kw), F32),
            pltpu.VMEM((CARRY_ROWS, kw), F32),
            pltpu.VMEM((CARRY_ROWS, vw), F32),
        ],
        compiler_params=_params("arbitrary", "arbitrary"),
        name="gdn_mixer",
    )(proj, proj, proj, proj, conv_w, conv_w, conv_w, g_row, g_col, b_col,
      head_norm.reshape(1, HEAD_DIM))


def _row_broadcast(x, first, period, rows):
    parts = [jnp.broadcast_to(x[r:r + 1], (period, x.shape[1])) for r in range(first, rows, period)]
    return jnp.concatenate(parts, axis=0)


def _hgrn_kernel(q_ref, f_ref, i_ref, g_ref, lb_ref, hn_ref, o_ref, s_ref):
    rows = q_ref.shape[0]
    n_groups = rows // GROUP
    n_chunks = rows // CHUNK
    per_group = GROUP // CHUNK

    @pl.when(pl.program_id(1) == 0)
    def _():
        s_ref[...] = jnp.zeros_like(s_ref)

    lbs = lb_ref[...]
    mx = jnp.max(lbs, axis=0, keepdims=True)
    ex = jnp.exp(lbs - mx)
    sm = ex / jnp.sum(ex, axis=0, keepdims=True)
    lb = (sm[0:1] + sm[1:2]) - sm[0:1]

    q = _silu(q_ref[...].astype(F32))
    fl = f_ref[...].astype(F32)
    v = i_ref[...].astype(F32)
    en = jnp.exp(-jnp.abs(fl))
    rc = 1.0 / (1.0 + en)
    pos = fl >= 0.0
    sig = jnp.where(pos, rc, en * rc)
    nsig = jnp.where(pos, en * rc, rc)
    logf = jnp.log(lb + (1.0 - lb) * sig)
    kk = (1.0 - lb) * nsig

    rowi = lax.broadcasted_iota(jnp.int32, (rows, HEAD_DIM), 0)
    row = lax.broadcasted_iota(jnp.int32, (GROUP, GROUP), 0)
    col = lax.broadcasted_iota(jnp.int32, (GROUP, GROUP), 1)

    tri = (((row // CHUNK) == (col // CHUNK)) & (row >= col)).astype(BF16)
    logf2 = logf * LOG2E
    hi = logf2.astype(BF16)
    rest = logf2 - hi.astype(F32)
    mid = rest.astype(BF16)
    lo = (rest - mid.astype(F32)).astype(BF16)
    pieces = jnp.concatenate([hi, mid, lo], axis=1)

    def mid_range(s):
        mid_row = (row // (2 * s)) * (2 * s) + (s - 1)
        return ((col > mid_row) & (col <= row)) | ((col > row) & (col <= mid_row))

    small_ranges = jnp.concatenate([mid_range(s).astype(BF16) for s in HGRN_SMALL_HALVES], axis=0)
    b_parts, small_parts = [], []
    for g in range(n_groups):
        pg = pieces[g * GROUP:(g + 1) * GROUP]
        s3 = jnp.dot(tri, pg, preferred_element_type=F32)
        b_parts.append(s3[:, :HEAD_DIM] + s3[:, HEAD_DIM:2 * HEAD_DIM] + s3[:, 2 * HEAD_DIM:])
        s2 = jnp.dot(small_ranges, pg[:, :2 * HEAD_DIM], preferred_element_type=F32)
        small_parts.append(s2[:, :HEAD_DIM] + s2[:, HEAD_DIM:])
    b = jnp.concatenate(b_parts, axis=0)
    b_last = _row_broadcast(b, CHUNK - 1, CHUNK, rows)
    qd = q * jnp.exp2(b)
    kd = kk * jnp.exp2(b_last - b)

    halves = HGRN_BIG_HALVES + HGRN_SMALL_HALVES
    level_x = []
    for s in HGRN_BIG_HALVES:
        b_mid = _row_broadcast(b, s - 1, 2 * s, rows)
        lower = (rowi % (2 * s)) >= s
        level_x.append(jnp.where(lower, q, kk) * jnp.exp2(-jnp.abs(b - b_mid)))
    for n, s in enumerate(HGRN_SMALL_HALVES):
        expo = jnp.concatenate([p[n * GROUP:(n + 1) * GROUP] for p in small_parts], axis=0)
        lower = (rowi % (2 * s)) >= s
        level_x.append(jnp.where(lower, q, kk) * jnp.exp2(expo))

    def rot8(x, d):
        return pltpu.roll(x.reshape(rows // 8, 8, HEAD_DIM), d, 1).reshape(rows, HEAD_DIM)

    diags = [jnp.sum(q * kk, axis=1, keepdims=True)]
    for d in range(1, HGRN_DIAG):
        e = jnp.exp2(jnp.minimum(b - rot8(b, d), 0.0))
        diags.append(jnp.sum(q * rot8(kk, d) * e, axis=1, keepdims=True))

    def grp(x, g):
        return x[g * GROUP:(g + 1) * GROUP]

    def chk(x, c):
        return x[c * CHUNK:(c + 1) * CHUNK]

    level_mm = [[_mm_nt(grp(x, g), grp(x, g)) for x in level_x] for g in range(n_groups)]
    kv = [_mm_tn(chk(v, c), chk(kd, c)) for c in range(n_chunks)]

    scores = []
    for g in range(n_groups):
        sc = jnp.zeros((GROUP, GROUP), F32)
        for s, mm in zip(halves, level_mm[g]):
            pair = ((row // (2 * s)) == (col // (2 * s))) & ((row % (2 * s)) >= s) & ((col % (2 * s)) < s)
            sc = jnp.where(pair, mm, sc)
        same_diag = (row // HGRN_DIAG) == (col // HGRN_DIAG)
        for d in range(HGRN_DIAG):
            sc = jnp.where(same_diag & (row - col == d), grp(diags[d], g), sc)
        scores.append(sc)

    state_t = s_ref[...]
    states = []
    for c in range(n_chunks):
        states.append(state_t)
        f_last = jnp.exp2(b[(c + 1) * CHUNK - 1:(c + 1) * CHUNK])
        state_t = state_t * f_last + kv[c]
    s_ref[...] = state_t

    intra = [_mm(scores[g], grp(v, g)) for g in range(n_groups)]
    inter = [_mm_nt(chk(qd, c), states[c]) for c in range(n_chunks)]
    hn = hn_ref[...]
    for c in range(n_chunks):
        g, i = divmod(c, per_group)
        o = inter[c] + intra[g][i * CHUNK:(i + 1) * CHUNK]
        gate = g_ref[c * CHUNK:(c + 1) * CHUNK, :].astype(F32)
        o_ref[c * CHUNK:(c + 1) * CHUNK, :] = _gated_rmsnorm(o, hn, gate).astype(o_ref.dtype)


def _hgrn_mixer(proj, lower_bounds, head_norm):
    m = proj.shape[0]
    rows = HGRN_ROWS
    nb = HGRN_DIM // HEAD_DIM
    return pl.pallas_call(
        _hgrn_kernel,
        grid=(HGRN_HEADS, m // rows),
        in_specs=[
            pl.BlockSpec((rows, HEAD_DIM), lambda h, t: (t, h)),
            pl.BlockSpec((rows, HEAD_DIM), lambda h, t: (t, nb + h)),
            pl.BlockSpec((rows, HEAD_DIM), lambda h, t: (t, 2 * nb + h)),
            pl.BlockSpec((rows, HEAD_DIM), lambda h, t: (t, 3 * nb + h)),
            pl.BlockSpec((2, HEAD_DIM), lambda h, t: (0, h)),
            pl.BlockSpec((1, HEAD_DIM), lambda h, t: (0, 0)),
        ],
        out_specs=pl.BlockSpec((rows, HEAD_DIM), lambda h, t: (t, h)),
        out_shape=jax.ShapeDtypeStruct((m, HGRN_DIM), BF16),
        scratch_shapes=[pltpu.VMEM((HEAD_DIM, HEAD_DIM), F32)],
        compiler_params=_params("arbitrary", "arbitrary"),
        name="hgrn_mixer",
    )(proj, proj, proj, proj, lower_bounds, head_norm.reshape(1, HEAD_DIM))


def _ffn(h, layer, norm_w, w_gate, w_up, w_down, hn=None):
    if hn is None:
        hn = _rmsnorm(h, norm_w[layer], BF16)
    act = _swiglu_up(hn, w_gate, w_up, layer, tm=1024, tn=512)
    return _proj_residual(act, w_down, layer, h, tm=512, tn=512)


def kernel(x, gdn_norm, gdn_w_in, gdn_conv, gdn_a_log, gdn_dt_bias, gdn_head_norm, gdn_w_out,
           hgrn_norm, hgrn_w_in, hgrn_lower_bounds, hgrn_head_norm, hgrn_w_out,
           ffn_norm, ffn_w_gate, ffn_w_up, ffn_w_down, final_norm):
    assert x.shape == (1, SEQ, D_MODEL)
    h = x.reshape(SEQ, D_MODEL)

    hn = _rmsnorm(h, gdn_norm[0], BF16)
    w_in_t = jnp.swapaxes(gdn_w_in, 1, 2)
    proj = _proj(hn, w_in_t, 0, GDN_MAIN_DIM, tm=2048, tn=1024, transposed=True)
    w_ba_t = w_in_t[0, GDN_MAIN_DIM:]
    g_row, b_row = _gdn_gates(hn, w_ba_t, gdn_a_log[0], gdn_dt_bias[0])
    mixed = _gdn_mixer(proj, gdn_conv[0], g_row, g_row.T, b_row.T, gdn_head_norm[0])
    h = _proj_residual(mixed, gdn_w_out, 0, h, tm=1024, tn=512)
    h = _ffn(h, 0, ffn_norm, ffn_w_gate, ffn_w_up, ffn_w_down)

    hn = _rmsnorm(h, hgrn_norm[0], BF16)
    proj = _proj(hn, hgrn_w_in, 0, 4 * HGRN_DIM, tm=2048, tn=1024)
    mixed = _hgrn_mixer(proj, hgrn_lower_bounds, hgrn_head_norm[0])
    h, hn = _proj_residual_norm(mixed, hgrn_w_out, 0, h, ffn_norm[1], tm=512)
    h = _ffn(h, 1, ffn_norm, ffn_w_gate, ffn_w_up, ffn_w_down, hn=hn)

    return _rmsnorm(h, final_norm, F32).reshape(1, SEQ, D_MODEL)
```

```python
import functools

import jax
import jax.numpy as jnp
from jax import lax
from jax.experimental import pallas as pl
from jax.experimental.pallas import tpu as pltpu

F32 = jnp.float32
BF16 = jnp.bfloat16

D_MODEL = 2048
SEQ = 8192
CHUNK = 64
HEAD_DIM = 128
EPS = 1e-6
LOG2E = 1.4426950408889634

GDN_K_HEADS = 16
GDN_V_HEADS = 32
GDN_KEY_DIM = GDN_K_HEADS * HEAD_DIM
GDN_VAL_DIM = GDN_V_HEADS * HEAD_DIM
GDN_CONV_DIM = 2 * GDN_KEY_DIM + GDN_VAL_DIM
GDN_MAIN_DIM = GDN_CONV_DIM + GDN_VAL_DIM
GDN_CONV_TAPS = 4

HGRN_HEADS = 16
HGRN_DIM = HGRN_HEADS * HEAD_DIM

VMEM_LIMIT_BYTES = 56 * 1024 * 1024
CARRY_ROWS = 8

GROUP = 2 * CHUNK
GDN_KH_PER_STEP = 16
GDN_ROWS = 128
HGRN_ROWS = 4096
HGRN_BIG_HALVES = (32, 16, 8)
HGRN_SMALL_HALVES = (4, 2)
HGRN_DIAG = 2
WEIGHT_CHUNK_ROWS = 512


def _params(*semantics):
    return pltpu.CompilerParams(dimension_semantics=semantics, vmem_limit_bytes=VMEM_LIMIT_BYTES)


def _mm(a, b):
    return jnp.dot(a.astype(BF16), b.astype(BF16), preferred_element_type=F32)


def _mm_nt(a, b):
    return lax.dot_general(a.astype(BF16), b.astype(BF16), (((1,), (1,)), ((), ())),
                           preferred_element_type=F32)


def _mm_tn(a, b):
    return lax.dot_general(a.astype(BF16), b.astype(BF16), (((0,), (0,)), ((), ())),
                           preferred_element_type=F32)


def _sigmoid(x):
    return 1.0 / (1.0 + jnp.exp(-x))


def _silu(x):
    h = 0.5 * x
    return h + h * jnp.tanh(h)


def _rmsnorm_kernel(x_ref, w_ref, o_ref):
    x = x_ref[...]
    y = x * lax.rsqrt(jnp.mean(x * x, axis=-1, keepdims=True) + EPS)
    o_ref[...] = (y * w_ref[...]).astype(o_ref.dtype)


def _rmsnorm(x, w, out_dtype, rows=1024):
    m, d = x.shape
    return pl.pallas_call(
        _rmsnorm_kernel,
        grid=(m // rows,),
        in_specs=[pl.BlockSpec((rows, d), lambda i: (i, 0)),
                  pl.BlockSpec((1, d), lambda i: (0, 0))],
        out_specs=pl.BlockSpec((rows, d), lambda i: (i, 0)),
        out_shape=jax.ShapeDtypeStruct((m, d), out_dtype),
        compiler_params=_params("arbitrary"),
        name="rmsnorm",
    )(x, w.reshape(1, d))


def _weight_spec(k, tn, layer, first_tile=0, transposed=False):
    if transposed:
        return pl.BlockSpec((None, tn, k), lambda j, i: (layer, first_tile + j, 0))
    return pl.BlockSpec((None, k, tn), lambda j, i: (layer, 0, first_tile + j))


def _load_weight(w_ref, wb_ref):
    w = w_ref[...]
    if w.shape != wb_ref.shape:
        w = w.T
    wb_ref[...] = w.astype(BF16)


def _proj_kernel(x_ref, w_ref, o_ref, wb_ref):
    @pl.when(pl.program_id(1) == 0)
    def _():
        _load_weight(w_ref, wb_ref)

    o_ref[...] = jnp.dot(x_ref[...], wb_ref[...], preferred_element_type=F32).astype(o_ref.dtype)


def _proj(x, w, layer, n_cols, tm, tn, first_col=0, transposed=False):
    m, k = x.shape
    return pl.pallas_call(
        _proj_kernel,
        grid=(n_cols // tn, m // tm),
        in_specs=[pl.BlockSpec((tm, k), lambda j, i: (i, 0)),
                  _weight_spec(k, tn, layer, first_col // tn, transposed)],
        out_specs=pl.BlockSpec((tm, tn), lambda j, i: (i, j)),
        out_shape=jax.ShapeDtypeStruct((m, n_cols), BF16),
        scratch_shapes=[pltpu.VMEM((k, tn), BF16)],
        compiler_params=_params("arbitrary", "arbitrary"),
        name="proj",
    )(x, w)


def _proj_rows_kernel(x_ref, w_ref, r_ref, nw_ref, *refs, n_load, keep_sum):
    wb_ref = refs[-1]
    s = pl.program_id(0)
    tk = w_ref.shape[0]

    @pl.when(s < n_load)
    def _():
        wb_ref[pl.ds(pl.multiple_of(s * tk, tk), tk), :] = w_ref[...].astype(BF16)

    @pl.when(s >= n_load)
    def _():
        h = r_ref[...] + jnp.dot(x_ref[...], wb_ref[...], preferred_element_type=F32)
        if keep_sum:
            refs[0][...] = h
        y = h * lax.rsqrt(jnp.mean(h * h, axis=-1, keepdims=True) + EPS)
        refs[-2][...] = (y * nw_ref[...]).astype(refs[-2].dtype)


def _proj_rows(x, w, layer, res, norm_w, tm, norm_dtype, keep_sum=True):
    m, k = x.shape
    n = w.shape[2]
    tk = WEIGHT_CHUNK_ROWS
    n_load = k // tk
    row = lambda s: (jnp.maximum(s - n_load, 0), 0)
    out_specs = [pl.BlockSpec((tm, n), row)]
    out_shape = [jax.ShapeDtypeStruct((m, n), norm_dtype)]
    if keep_sum:
        out_specs.insert(0, pl.BlockSpec((tm, n), row))
        out_shape.insert(0, jax.ShapeDtypeStruct((m, n), F32))
    return pl.pallas_call(
        functools.partial(_proj_rows_kernel, n_load=n_load, keep_sum=keep_sum),
        grid=(n_load + m // tm,),
        in_specs=[pl.BlockSpec((tm, k), row),
                  pl.BlockSpec((None, tk, n), lambda s: (layer, jnp.minimum(s, n_load - 1), 0)),
                  pl.BlockSpec((tm, n), row),
                  pl.BlockSpec((1, n), lambda s: (0, 0))],
        out_specs=out_specs,
        out_shape=out_shape,
        scratch_shapes=[pltpu.VMEM((k, n), BF16)],
        compiler_params=_params("arbitrary"),
        name="proj_rows",
    )(x, w, res, norm_w.reshape(1, n))


def _swiglu_up_kernel(x_ref, wg_ref, wu_ref, o_ref, wgb_ref, wub_ref):
    @pl.when(pl.program_id(1) == 0)
    def _():
        wgb_ref[...] = wg_ref[...].astype(BF16)
        wub_ref[...] = wu_ref[...].astype(BF16)

    x = x_ref[...]
    g = jnp.dot(x, wgb_ref[...], preferred_element_type=F32)
    u = jnp.dot(x, wub_ref[...], preferred_element_type=F32)
    o_ref[...] = (_silu(g) * u).astype(o_ref.dtype)


def _swiglu_up(x, w_gate, w_up, layer, tm, tn):
    m, k = x.shape
    n = w_gate.shape[2]
    return pl.pallas_call(
        _swiglu_up_kernel,
        grid=(n // tn, m // tm),
        in_specs=[pl.BlockSpec((tm, k), lambda j, i: (i, 0)), _weight_spec(k, tn, layer),
                  _weight_spec(k, tn, layer)],
        out_specs=pl.BlockSpec((tm, tn), lambda j, i: (i, j)),
        out_shape=jax.ShapeDtypeStruct((m, n), BF16),
        scratch_shapes=[pltpu.VMEM((k, tn), BF16), pltpu.VMEM((k, tn), BF16)],
        compiler_params=_params("arbitrary", "arbitrary"),
        name="swiglu_up",
    )(x, w_gate, w_up)


def _gdn_gate_kernel(x_ref, w_ref, alog_ref, dtb_ref, g_ref, beta_ref):
    ba = _mm_nt(w_ref[...], x_ref[...])
    nh = GDN_V_HEADS
    beta_ref[...] = _sigmoid(ba[:nh])
    a = ba[nh:] + dtb_ref[...]
    softplus = jnp.maximum(a, 0.0) + jnp.log1p(jnp.exp(-jnp.abs(a)))
    g = (-LOG2E * jnp.exp(alog_ref[...])) * softplus
    pos = lax.broadcasted_iota(jnp.int32, g.shape, 1) % CHUNK
    shift = 1
    while shift < CHUNK:
        g = g + jnp.where(pos >= shift, pltpu.roll(g, shift, 1), 0.0)
        shift *= 2
    g_ref[...] = g


def _gdn_gates(x, w_ba_t, a_log, dt_bias, rows=512):
    m, k = x.shape
    nh = GDN_V_HEADS
    return pl.pallas_call(
        _gdn_gate_kernel,
        grid=(m // rows,),
        in_specs=[pl.BlockSpec((rows, k), lambda i: (i, 0)),
                  pl.BlockSpec((2 * nh, k), lambda i: (0, 0)),
                  pl.BlockSpec((nh, 1), lambda i: (0, 0)),
                  pl.BlockSpec((nh, 1), lambda i: (0, 0))],
        out_specs=[pl.BlockSpec((nh, rows), lambda i: (0, i)),
                   pl.BlockSpec((nh, rows), lambda i: (0, i))],
        out_shape=[jax.ShapeDtypeStruct((nh, m), F32), jax.ShapeDtypeStruct((nh, m), F32)],
        compiler_params=_params("arbitrary"),
        name="gdn_gates",
    )(x, w_ba_t, a_log.reshape(nh, 1), dt_bias.reshape(nh, 1))


def _conv_silu(x_ref, carry_ref, w_ref):
    x = x_ref[...].astype(F32)
    rows, width = x.shape
    xe = jnp.concatenate([carry_ref[...], x], axis=0).reshape(rows // 8 + 1, 8, width)
    in_tile = lax.broadcasted_iota(jnp.int32, (1, 8, width), 1)
    w = w_ref[...]
    taps = GDN_CONV_TAPS
    y = w[taps - 1:taps] * x
    for j in range(1, taps):
        r = pltpu.roll(xe, j, 1)
        shifted = jnp.where(in_tile < j, r[:-1], r[1:]).reshape(rows, width)
        y = y + w[taps - 1 - j:taps - j] * shifted
    carry_ref[...] = x[rows - CARRY_ROWS:]
    return _silu(y)


def _l2norm(x):
    return x * lax.rsqrt(jnp.sum(x * x, axis=-1, keepdims=True) + 1e-6)


def _gated_rmsnorm(o, w, z):
    y = o * lax.rsqrt(jnp.mean(o * o, axis=-1, keepdims=True) + EPS)
    return y * w * _silu(z)


def _unit_lower_inverses(a_list, row, col):
    eye = (row == col).astype(F32)
    same16 = (row // 16) == (col // 16)
    same32 = (row // 32) == (col // 32)
    off32 = same32 & jnp.logical_not(same16)
    off64 = jnp.logical_not(same32)
    ad = [jnp.where(same16, a, 0.0) for a in a_list]
    n = row.shape[0]
    p = [_mm(x, x) for x in ad]
    t = [eye - x for x in ad]
    for _ in range(2):
        both = [_mm(jnp.concatenate([x, y], axis=0), y) for x, y in zip(t, p)]
        t = [x + b[:n] for x, b in zip(t, both)]
        p = [b[n:] for b in both]
    t = [x + _mm(x, y) for x, y in zip(t, p)]
    for mask in (off32, off64):
        y = [_mm(jnp.where(mask, a, 0.0), x) for a, x in zip(a_list, t)]
        t = [x - _mm(x, yy) for x, yy in zip(t, y)]
    return t


def _gdn_kernel(q_ref, k_ref, v_ref, z_ref, wq_ref, wk_ref, wv_ref, grow_ref,
                gcol_ref, bcol_ref, hn_ref, o_ref, s_ref, cq_ref, ck_ref, cv_ref):
    hb = pl.program_id(0)
    rows = q_ref.shape[0]
    n_groups = rows // GROUP
    n_chunks = rows // CHUNK
    per_group = GROUP // CHUNK

    @pl.when(pl.program_id(1) == 0)
    def _():
        s_ref[...] = jnp.zeros_like(s_ref)
        cq_ref[...] = jnp.zeros_like(cq_ref)
        ck_ref[...] = jnp.zeros_like(ck_ref)
        cv_ref[...] = jnp.zeros_like(cv_ref)

    q_all = _conv_silu(q_ref, cq_ref, wq_ref)
    k_all = _conv_silu(k_ref, ck_ref, wk_ref)
    v_all = _conv_silu(v_ref, cv_ref, wv_ref)
    hn = hn_ref[...]

    def lanes(x, i):
        return x[:, i * HEAD_DIM:(i + 1) * HEAD_DIM]

    qs = [_l2norm(lanes(q_all, j)) * (HEAD_DIM ** -0.5) for j in range(GDN_KH_PER_STEP)]
    ks = [_l2norm(lanes(k_all, j)) for j in range(GDN_KH_PER_STEP)]

    row = lax.broadcasted_iota(jnp.int32, (GROUP, GROUP), 0)
    col = lax.broadcasted_iota(jnp.int32, (GROUP, GROUP), 1)
    same_chunk = (row // CHUNK) == (col // CHUNK)
    causal = (row >= col) & same_chunk
    strict = row > col

    heads = [(j, e) for j in range(GDN_KH_PER_STEP) for e in range(2)]
    head_lane = lax.broadcasted_iota(jnp.int32, gcol_ref.shape, 1)
    gcol_all = gcol_ref[...]
    bcol_all = bcol_ref[...]
    g_col, b_col, g_row = [], [], []
    for j, e in heads:
        head = 2 * (GDN_KH_PER_STEP * hb + j) + e
        sel = head_lane == head
        g_col.append(jnp.sum(jnp.where(sel, gcol_all, 0.0), axis=1, keepdims=True))
        b_col.append(jnp.sum(jnp.where(sel, bcol_all, 0.0), axis=1, keepdims=True))
        g_row.append(grow_ref[pl.ds(head, 1), :])

    def grp(x, g):
        return x[g * GROUP:(g + 1) * GROUP]

    kq = [[_mm_nt(jnp.concatenate([grp(qs[j], g), grp(ks[j], g)], axis=0), grp(ks[j], g))
           for g in range(n_groups)] for j in range(GDN_KH_PER_STEP)]

    items = [(h, g) for h in range(len(heads)) for g in range(n_groups)]
    a_list, aqk_list, rhs_list, qd_list, kd_list = [], [], [], [], []
    for h, g in items:
        j, e = heads[h]
        gc = grp(g_col[h], g)
        bc = grp(b_col[h], g)
        gr = g_row[h][:, g * GROUP:(g + 1) * GROUP]
        decay = jnp.exp2(jnp.where(causal, gc - gr, -jnp.inf))
        a_list.append(jnp.where(strict, bc * kq[j][g][GROUP:] * decay, 0.0))
        aqk_list.append(kq[j][g][:GROUP] * decay)
        eg = jnp.exp2(gc)
        kg = grp(ks[j], g)
        vg = grp(lanes(v_all, 2 * j + e), g)
        rhs_list.append(jnp.concatenate([vg * bc, kg * (bc * eg)], axis=1))
        g_last = jnp.concatenate(
            [jnp.broadcast_to(gc[(i + 1) * CHUNK - 1:(i + 1) * CHUNK], (CHUNK, 1)) for i in range(per_group)],
            axis=0)
        qd_list.append(grp(qs[j], g) * eg)
        kd_list.append(kg * jnp.exp2(g_last - gc))

    t_list = _unit_lower_inverses(a_list, row, col)
    uw_list = [_mm(t, r) for t, r in zip(t_list, rhs_list)]

    states = [s_ref[h] for h in range(len(heads))]
    for c in range(n_chunks):
        g, i = divmod(c, per_group)
        sl = slice(i * CHUNK, (i + 1) * CHUNK)
        idx = [h * n_groups + g for h in range(len(heads))]
        ws_qs = [_mm(jnp.concatenate([uw_list[n][sl, HEAD_DIM:], qd_list[n][sl]], axis=0), states[h])
                 for h, n in enumerate(idx)]
        v_new = [uw_list[n][sl, :HEAD_DIM] - r[:CHUNK] for n, r in zip(idx, ws_qs)]
        intra = [_mm(aqk_list[n][sl, i * CHUNK:(i + 1) * CHUNK], vn) for n, vn in zip(idx, v_new)]
        upd = [_mm_tn(kd_list[n][sl], vn) for n, vn in zip(idx, v_new)]
        for h in range(len(heads)):
            j, e = heads[h]
            g_last = g_col[h][(c + 1) * CHUNK - 1:(c + 1) * CHUNK]
            states[h] = states[h] * jnp.exp2(g_last) + upd[h]
            o = ws_qs[h][CHUNK:] + intra[h]
            z = z_ref[c * CHUNK:(c + 1) * CHUNK, (2 * j + e) * HEAD_DIM:(2 * j + e + 1) * HEAD_DIM].astype(F32)
            o_ref[c * CHUNK:(c + 1) * CHUNK, (2 * j + e) * HEAD_DIM:(2 * j + e + 1) * HEAD_DIM] = (
                _gated_rmsnorm(o, hn, z).astype(o_ref.dtype))
    for h in range(len(heads)):
        s_ref[h] = states[h]


def _gdn_mixer(proj, conv_w, g_row, g_col, b_col, head_norm):
    m = proj.shape[0]
    rows = GDN_ROWS
    nh = GDN_V_HEADS
    kw = GDN_KH_PER_STEP * HEAD_DIM
    vw = 2 * kw
    kb = GDN_KEY_DIM // kw
    vb = 2 * GDN_KEY_DIM // vw
    zb = GDN_CONV_DIM // vw
    return pl.pallas_call(
        _gdn_kernel,
        grid=(GDN_K_HEADS // GDN_KH_PER_STEP, m // rows),
        in_specs=[
            pl.BlockSpec((rows, kw), lambda h, t: (t, h)),
            pl.BlockSpec((rows, kw), lambda h, t: (t, kb + h)),
            pl.BlockSpec((rows, vw), lambda h, t: (t, vb + h)),
            pl.BlockSpec((rows, vw), lambda h, t: (t, zb + h)),
            pl.BlockSpec((GDN_CONV_TAPS, kw), lambda h, t: (0, h)),
            pl.BlockSpec((GDN_CONV_TAPS, kw), lambda h, t: (0, kb + h)),
            pl.BlockSpec((GDN_CONV_TAPS, vw), lambda h, t: (0, vb + h)),
            pl.BlockSpec((nh, rows), lambda h, t: (0, t)),
            pl.BlockSpec((rows, nh), lambda h, t: (t, 0)),
            pl.BlockSpec((rows, nh), lambda h, t: (t, 0)),
            pl.BlockSpec((1, HEAD_DIM), lambda h, t: (0, 0)),
        ],
        out_specs=pl.BlockSpec((rows, vw), lambda h, t: (t, h)),
        out_shape=jax.ShapeDtypeStruct((m, GDN_VAL_DIM), BF16),
        scratch_shapes=[
            pltpu.VMEM((2 * GDN_KH_PER_STEP, HEAD_DIM, HEAD_DIM), F32),
            pltpu.VMEM((CARRY_ROWS, kw), F32),
            pltpu.VMEM((CARRY_ROWS, kw), F32),
            pltpu.VMEM((CARRY_ROWS, vw), F32),
        ],
        compiler_params=_params("arbitrary", "arbitrary"),
        name="gdn_mixer",
    )(proj, proj, proj, proj, conv_w, conv_w, conv_w, g_row, g_col, b_col,
      head_norm.reshape(1, HEAD_DIM))


def _row_broadcast(x, first, period, rows):
    parts = [jnp.broadcast_to(x[r:r + 1], (period, x.shape[1])) for r in range(first, rows, period)]
    return jnp.concatenate(parts, axis=0)


def _hgrn_kernel(q_ref, f_ref, i_ref, g_ref, lb_ref, hn_ref, o_ref, s_ref):
    rows = q_ref.shape[0]
    n_groups = rows // GROUP
    n_chunks = rows // CHUNK
    per_group = GROUP // CHUNK

    @pl.when(pl.program_id(1) == 0)
    def _():
        s_ref[...] = jnp.zeros_like(s_ref)

    lbs = lb_ref[...]
    mx = jnp.max(lbs, axis=0, keepdims=True)
    ex = jnp.exp(lbs - mx)
    sm = ex / jnp.sum(ex, axis=0, keepdims=True)
    lb = (sm[0:1] + sm[1:2]) - sm[0:1]

    q = _silu(q_ref[...].astype(F32))
    fl = f_ref[...].astype(F32)
    v = i_ref[...].astype(F32)
    en = jnp.exp(-jnp.abs(fl))
    rc = 1.0 / (1.0 + en)
    pos = fl >= 0.0
    sig = jnp.where(pos, rc, en * rc)
    nsig = jnp.where(pos, en * rc, rc)
    logf = jnp.log(lb + (1.0 - lb) * sig)
    kk = (1.0 - lb) * nsig

    rowi = lax.broadcasted_iota(jnp.int32, (rows, HEAD_DIM), 0)
    row = lax.broadcasted_iota(jnp.int32, (GROUP, GROUP), 0)
    col = lax.broadcasted_iota(jnp.int32, (GROUP, GROUP), 1)

    tri = (((row // CHUNK) == (col // CHUNK)) & (row >= col)).astype(BF16)
    logf2 = logf * LOG2E
    hi = logf2.astype(BF16)
    rest = logf2 - hi.astype(F32)
    mid = rest.astype(BF16)
    lo = (rest - mid.astype(F32)).astype(BF16)
    pieces = jnp.concatenate([hi, mid, lo], axis=1)

    def mid_range(s):
        mid_row = (row // (2 * s)) * (2 * s) + (s - 1)
        return ((col > mid_row) & (col <= row)) | ((col > row) & (col <= mid_row))

    small_ranges = jnp.concatenate([mid_range(s).astype(BF16) for s in HGRN_SMALL_HALVES], axis=0)
    b_parts, small_parts = [], []
    for g in range(n_groups):
        pg = pieces[g * GROUP:(g + 1) * GROUP]
        s3 = jnp.dot(tri, pg, preferred_element_type=F32)
        b_parts.append(s3[:, :HEAD_DIM] + s3[:, HEAD_DIM:2 * HEAD_DIM] + s3[:, 2 * HEAD_DIM:])
        s2 = jnp.dot(small_ranges, pg[:, :2 * HEAD_DIM], preferred_element_type=F32)
        small_parts.append(s2[:, :HEAD_DIM] + s2[:, HEAD_DIM:])
    b = jnp.concatenate(b_parts, axis=0)
    b_last = _row_broadcast(b, CHUNK - 1, CHUNK, rows)
    qd = q * jnp.exp2(b)
    kd = kk * jnp.exp2(b_last - b)

    halves = HGRN_BIG_HALVES + HGRN_SMALL_HALVES
    level_x = []
    for s in HGRN_BIG_HALVES:
        b_mid = _row_broadcast(b, s - 1, 2 * s, rows)
        lower = (rowi % (2 * s)) >= s
        level_x.append(jnp.where(lower, q, kk) * jnp.exp2(-jnp.abs(b - b_mid)))
    for n, s in enumerate(HGRN_SMALL_HALVES):
        expo = jnp.concatenate([p[n * GROUP:(n + 1) * GROUP] for p in small_parts], axis=0)
        lower = (rowi % (2 * s)) >= s
        level_x.append(jnp.where(lower, q, kk) * jnp.exp2(expo))

    def rot8(x, d):
        return pltpu.roll(x.reshape(rows // 8, 8, HEAD_DIM), d, 1).reshape(rows, HEAD_DIM)

    diags = [jnp.sum(q * kk, axis=1, keepdims=True)]
    for d in range(1, HGRN_DIAG):
        e = jnp.exp2(jnp.minimum(b - rot8(b, d), 0.0))
        diags.append(jnp.sum(q * rot8(kk, d) * e, axis=1, keepdims=True))

    def grp(x, g):
        return x[g * GROUP:(g + 1) * GROUP]

    def chk(x, c):
        return x[c * CHUNK:(c + 1) * CHUNK]

    level_mm = [[_mm_nt(grp(x, g), grp(x, g)) for x in level_x] for g in range(n_groups)]
    kv = [_mm_tn(chk(v, c), chk(kd, c)) for c in range(n_chunks)]

    scores = []
    for g in range(n_groups):
        sc = jnp.zeros((GROUP, GROUP), F32)
        for s, mm in zip(halves, level_mm[g]):
            pair = ((row // (2 * s)) == (col // (2 * s))) & ((row % (2 * s)) >= s) & ((col % (2 * s)) < s)
            sc = jnp.where(pair, mm, sc)
        same_diag = (row // HGRN_DIAG) == (col // HGRN_DIAG)
        for d in range(HGRN_DIAG):
            sc = jnp.where(same_diag & (row - col == d), grp(diags[d], g), sc)
        scores.append(sc)

    state_t = s_ref[...]
    states = []
    for c in range(n_chunks):
        states.append(state_t)
        f_last = jnp.exp2(b[(c + 1) * CHUNK - 1:(c + 1) * CHUNK])
        state_t = state_t * f_last + kv[c]
    s_ref[...] = state_t

    intra = [_mm(scores[g], grp(v, g)) for g in range(n_groups)]
    inter = [_mm_nt(chk(qd, c), states[c]) for c in range(n_chunks)]
    hn = hn_ref[...]
    for c in range(n_chunks):
        g, i = divmod(c, per_group)
        o = inter[c] + intra[g][i * CHUNK:(i + 1) * CHUNK]
        gate = g_ref[c * CHUNK:(c + 1) * CHUNK, :].astype(F32)
        o_ref[c * CHUNK:(c + 1) * CHUNK, :] = _gated_rmsnorm(o, hn, gate).astype(o_ref.dtype)


def _hgrn_mixer(proj, lower_bounds, head_norm):
    m = proj.shape[0]
    rows = HGRN_ROWS
    nb = HGRN_DIM // HEAD_DIM
    return pl.pallas_call(
        _hgrn_kernel,
        grid=(HGRN_HEADS, m // rows),
        in_specs=[
            pl.BlockSpec((rows, HEAD_DIM), lambda h, t: (t, h)),
            pl.BlockSpec((rows, HEAD_DIM), lambda h, t: (t, nb + h)),
            pl.BlockSpec((rows, HEAD_DIM), lambda h, t: (t, 2 * nb + h)),
            pl.BlockSpec((rows, HEAD_DIM), lambda h, t: (t, 3 * nb + h)),
            pl.BlockSpec((2, HEAD_DIM), lambda h, t: (0, h)),
            pl.BlockSpec((1, HEAD_DIM), lambda h, t: (0, 0)),
        ],
        out_specs=pl.BlockSpec((rows, HEAD_DIM), lambda h, t: (t, h)),
        out_shape=jax.ShapeDtypeStruct((m, HGRN_DIM), BF16),
        scratch_shapes=[pltpu.VMEM((HEAD_DIM, HEAD_DIM), F32)],
        compiler_params=_params("arbitrary", "arbitrary"),
        name="hgrn_mixer",
    )(proj, proj, proj, proj, lower_bounds, head_norm.reshape(1, HEAD_DIM))


def kernel(x, gdn_norm, gdn_w_in, gdn_conv, gdn_a_log, gdn_dt_bias, gdn_head_norm, gdn_w_out,
           hgrn_norm, hgrn_w_in, hgrn_lower_bounds, hgrn_head_norm, hgrn_w_out,
           ffn_norm, ffn_w_gate, ffn_w_up, ffn_w_down, final_norm):
    assert x.shape == (1, SEQ, D_MODEL)
    h = x.reshape(SEQ, D_MODEL)

    hn = _rmsnorm(h, gdn_norm[0], BF16)
    w_in_t = jnp.swapaxes(gdn_w_in, 1, 2)
    proj = _proj(hn, w_in_t, 0, GDN_MAIN_DIM, tm=2048, tn=1024, transposed=True)
    w_ba_t = w_in_t[0, GDN_MAIN_DIM:]
    g_row, b_row = _gdn_gates(hn, w_ba_t, gdn_a_log[0], gdn_dt_bias[0])
    mixed = _gdn_mixer(proj, gdn_conv[0], g_row, g_row.T, b_row.T, gdn_head_norm[0])
    h, hn = _proj_rows(mixed, gdn_w_out, 0, h, ffn_norm[0], 256, BF16)
    act = _swiglu_up(hn, ffn_w_gate, ffn_w_up, 0, tm=1024, tn=512)
    h, hn = _proj_rows(act, ffn_w_down, 0, h, hgrn_norm[0], 256, BF16)

    proj = _proj(hn, hgrn_w_in, 0, 4 * HGRN_DIM, tm=2048, tn=1024)
    mixed = _hgrn_mixer(proj, hgrn_lower_bounds, hgrn_head_norm[0])
    h, hn = _proj_rows(mixed, hgrn_w_out, 0, h, ffn_norm[1], 512, BF16)
    act = _swiglu_up(hn, ffn_w_gate, ffn_w_up, 1, tm=1024, tn=512)
    (out,) = _proj_rows(act, ffn_w_down, 1, h, final_norm, 256, F32, keep_sum=False)
    return out.reshape(1, SEQ, D_MODEL)
```

```python
import functools

import jax
import jax.numpy as jnp
from jax import lax
from jax.experimental import pallas as pl
from jax.experimental.pallas import tpu as pltpu

F32 = jnp.float32
BF16 = jnp.bfloat16

D_MODEL = 2048
SEQ = 8192
CHUNK = 64
HEAD_DIM = 128
EPS = 1e-6
LOG2E = 1.4426950408889634

GDN_K_HEADS = 16
GDN_V_HEADS = 32
GDN_KEY_DIM = GDN_K_HEADS * HEAD_DIM
GDN_VAL_DIM = GDN_V_HEADS * HEAD_DIM
GDN_CONV_DIM = 2 * GDN_KEY_DIM + GDN_VAL_DIM
GDN_MAIN_DIM = GDN_CONV_DIM + GDN_VAL_DIM
GDN_CONV_TAPS = 4

HGRN_HEADS = 16
HGRN_DIM = HGRN_HEADS * HEAD_DIM

VMEM_LIMIT_BYTES = 56 * 1024 * 1024
CARRY_ROWS = 8

GROUP = 2 * CHUNK
GDN_KH_PER_STEP = 16
GDN_ROWS = 128
HGRN_ROWS = 4096
HGRN_BIG_HALVES = (32, 16, 8)
HGRN_SMALL_HALVES = (4, 2)
HGRN_DIAG = 2
WEIGHT_CHUNK_ROWS = 512


def _params(*semantics):
    return pltpu.CompilerParams(dimension_semantics=semantics, vmem_limit_bytes=VMEM_LIMIT_BYTES)


def _mm(a, b):
    return jnp.dot(a.astype(BF16), b.astype(BF16), preferred_element_type=F32)


def _mm_nt(a, b):
    return lax.dot_general(a.astype(BF16), b.astype(BF16), (((1,), (1,)), ((), ())),
                           preferred_element_type=F32)


def _mm_tn(a, b):
    return lax.dot_general(a.astype(BF16), b.astype(BF16), (((0,), (0,)), ((), ())),
                           preferred_element_type=F32)


def _sigmoid(x):
    return 1.0 / (1.0 + jnp.exp(-x))


def _silu(x):
    h = 0.5 * x
    return h + h * jnp.tanh(h)


def _weight_spec(k, tn, layer, first_tile=0, transposed=False):
    if transposed:
        return pl.BlockSpec((None, tn, k), lambda j, i: (layer, first_tile + j, 0))
    return pl.BlockSpec((None, k, tn), lambda j, i: (layer, 0, first_tile + j))


def _load_weight(w_ref, wb_ref):
    w = w_ref[...]
    if w.shape != wb_ref.shape:
        w = w.T
    wb_ref[...] = w.astype(BF16)


def _proj_kernel(x_ref, w_ref, o_ref, wb_ref):
    @pl.when(pl.program_id(1) == 0)
    def _():
        _load_weight(w_ref, wb_ref)

    o_ref[...] = jnp.dot(x_ref[...], wb_ref[...], preferred_element_type=F32).astype(o_ref.dtype)


def _proj(x, w, layer, n_cols, tm, tn, first_col=0, transposed=False):
    m, k = x.shape
    return pl.pallas_call(
        _proj_kernel,
        grid=(n_cols // tn, m // tm),
        in_specs=[pl.BlockSpec((tm, k), lambda j, i: (i, 0)),
                  _weight_spec(k, tn, layer, first_col // tn, transposed)],
        out_specs=pl.BlockSpec((tm, tn), lambda j, i: (i, j)),
        out_shape=jax.ShapeDtypeStruct((m, n_cols), BF16),
        scratch_shapes=[pltpu.VMEM((k, tn), BF16)],
        compiler_params=_params("arbitrary", "arbitrary"),
        name="proj",
    )(x, w)


def _proj_rows_kernel(x_ref, w_ref, r_ref, nw_ref, *refs, n_load, keep_sum):
    wb_ref = refs[-1]
    s = pl.program_id(0)
    tk = w_ref.shape[0]

    @pl.when(s < n_load)
    def _():
        wb_ref[pl.ds(pl.multiple_of(s * tk, tk), tk), :] = w_ref[...].astype(BF16)

    @pl.when(s >= n_load)
    def _():
        h = r_ref[...] + jnp.dot(x_ref[...], wb_ref[...], preferred_element_type=F32)
        if keep_sum:
            refs[0][...] = h
        y = h * lax.rsqrt(jnp.mean(h * h, axis=-1, keepdims=True) + EPS)
        refs[-2][...] = (y * nw_ref[...]).astype(refs[-2].dtype)


def _proj_rows(x, w, layer, res, norm_w, tm, norm_dtype, keep_sum=True):
    m, k = x.shape
    n = w.shape[2]
    tk = WEIGHT_CHUNK_ROWS
    n_load = k // tk
    row = lambda s: (jnp.maximum(s - n_load, 0), 0)
    out_specs = [pl.BlockSpec((tm, n), row)]
    out_shape = [jax.ShapeDtypeStruct((m, n), norm_dtype)]
    if keep_sum:
        out_specs.insert(0, pl.BlockSpec((tm, n), row))
        out_shape.insert(0, jax.ShapeDtypeStruct((m, n), F32))
    return pl.pallas_call(
        functools.partial(_proj_rows_kernel, n_load=n_load, keep_sum=keep_sum),
        grid=(n_load + m // tm,),
        in_specs=[pl.BlockSpec((tm, k), row),
                  pl.BlockSpec((None, tk, n), lambda s: (layer, jnp.minimum(s, n_load - 1), 0)),
                  pl.BlockSpec((tm, n), row),
                  pl.BlockSpec((1, n), lambda s: (0, 0))],
        out_specs=out_specs,
        out_shape=out_shape,
        scratch_shapes=[pltpu.VMEM((k, n), BF16)],
        compiler_params=_params("arbitrary"),
        name="proj_rows",
    )(x, w, res, norm_w.reshape(1, n))


def _swiglu_up_kernel(x_ref, wg_ref, wu_ref, o_ref, wgb_ref, wub_ref):
    @pl.when(pl.program_id(1) == 0)
    def _():
        wgb_ref[...] = wg_ref[...].astype(BF16)
        wub_ref[...] = wu_ref[...].astype(BF16)

    x = x_ref[...]
    g = jnp.dot(x, wgb_ref[...], preferred_element_type=F32)
    u = jnp.dot(x, wub_ref[...], preferred_element_type=F32)
    o_ref[...] = (_silu(g) * u).astype(o_ref.dtype)


def _swiglu_up(x, w_gate, w_up, layer, tm, tn):
    m, k = x.shape
    n = w_gate.shape[2]
    return pl.pallas_call(
        _swiglu_up_kernel,
        grid=(n // tn, m // tm),
        in_specs=[pl.BlockSpec((tm, k), lambda j, i: (i, 0)), _weight_spec(k, tn, layer),
                  _weight_spec(k, tn, layer)],
        out_specs=pl.BlockSpec((tm, tn), lambda j, i: (i, j)),
        out_shape=jax.ShapeDtypeStruct((m, n), BF16),
        scratch_shapes=[pltpu.VMEM((k, tn), BF16), pltpu.VMEM((k, tn), BF16)],
        compiler_params=_params("arbitrary", "arbitrary"),
        name="swiglu_up",
    )(x, w_gate, w_up)


def _gdn_gate_kernel(x_ref, nw_ref, w_ref, alog_ref, dtb_ref, hn_ref, g_ref, beta_ref):
    x = x_ref[...]
    hn = (x * lax.rsqrt(jnp.mean(x * x, axis=-1, keepdims=True) + EPS) * nw_ref[...]).astype(BF16)
    hn_ref[...] = hn
    ba = _mm_nt(w_ref[...], hn)
    nh = GDN_V_HEADS
    beta_ref[...] = _sigmoid(ba[:nh])
    a = ba[nh:] + dtb_ref[...]
    softplus = jnp.maximum(a, 0.0) + jnp.log1p(jnp.exp(-jnp.abs(a)))
    g = (-LOG2E * jnp.exp(alog_ref[...])) * softplus
    pos = lax.broadcasted_iota(jnp.int32, g.shape, 1) % CHUNK
    shift = 1
    while shift < CHUNK:
        g = g + jnp.where(pos >= shift, pltpu.roll(g, shift, 1), 0.0)
        shift *= 2
    g_ref[...] = g


def _gdn_gates(x, norm_w, w_ba_t, a_log, dt_bias, rows=512):
    m, k = x.shape
    nh = GDN_V_HEADS
    return pl.pallas_call(
        _gdn_gate_kernel,
        grid=(m // rows,),
        in_specs=[pl.BlockSpec((rows, k), lambda i: (i, 0)),
                  pl.BlockSpec((1, k), lambda i: (0, 0)),
                  pl.BlockSpec((2 * nh, k), lambda i: (0, 0)),
                  pl.BlockSpec((nh, 1), lambda i: (0, 0)),
                  pl.BlockSpec((nh, 1), lambda i: (0, 0))],
        out_specs=[pl.BlockSpec((rows, k), lambda i: (i, 0)),
                   pl.BlockSpec((nh, rows), lambda i: (0, i)),
                   pl.BlockSpec((nh, rows), lambda i: (0, i))],
        out_shape=[jax.ShapeDtypeStruct((m, k), BF16),
                   jax.ShapeDtypeStruct((nh, m), F32), jax.ShapeDtypeStruct((nh, m), F32)],
        compiler_params=_params("arbitrary"),
        name="gdn_gates",
    )(x, norm_w.reshape(1, k), w_ba_t, a_log.reshape(nh, 1), dt_bias.reshape(nh, 1))


def _conv_silu(x_ref, carry_ref, w_ref):
    x = x_ref[...].astype(F32)
    rows, width = x.shape
    xe = jnp.concatenate([carry_ref[...], x], axis=0).reshape(rows // 8 + 1, 8, width)
    in_tile = lax.broadcasted_iota(jnp.int32, (1, 8, width), 1)
    w = w_ref[...]
    taps = GDN_CONV_TAPS
    y = w[taps - 1:taps] * x
    for j in range(1, taps):
        r = pltpu.roll(xe, j, 1)
        shifted = jnp.where(in_tile < j, r[:-1], r[1:]).reshape(rows, width)
        y = y + w[taps - 1 - j:taps - j] * shifted
    carry_ref[...] = x[rows - CARRY_ROWS:]
    return _silu(y)


def _l2norm(x, scale=1.0):
    return x * (lax.rsqrt(jnp.sum(x * x, axis=-1, keepdims=True) + 1e-6) * scale)


def _gated_rmsnorm(o, w, z):
    y = o * lax.rsqrt(jnp.mean(o * o, axis=-1, keepdims=True) + EPS)
    return y * w * _silu(z)


def _unit_lower_inverses(a_list, row, col):
    eye = (row == col).astype(F32)
    same16 = (row // 16) == (col // 16)
    same32 = (row // 32) == (col // 32)
    off32 = same32 & jnp.logical_not(same16)
    off64 = jnp.logical_not(same32)
    ad = [jnp.where(same16, a, 0.0) for a in a_list]
    n = row.shape[0]
    p = [_mm(x, x) for x in ad]
    t = [eye - x for x in ad]
    for _ in range(2):
        both = [_mm(jnp.concatenate([x, y], axis=0), y) for x, y in zip(t, p)]
        t = [x + b[:n] for x, b in zip(t, both)]
        p = [b[n:] for b in both]
    t = [x + _mm(x, y) for x, y in zip(t, p)]
    for mask in (off32, off64):
        y = [_mm(jnp.where(mask, a, 0.0), x) for a, x in zip(a_list, t)]
        t = [x - _mm(x, yy) for x, yy in zip(t, y)]
    return t


def _gdn_kernel(q_ref, k_ref, v_ref, z_ref, wq_ref, wk_ref, wv_ref, grow_ref,
                gcol_ref, bcol_ref, hn_ref, o_ref, s_ref, cq_ref, ck_ref, cv_ref):
    hb = pl.program_id(0)
    rows = q_ref.shape[0]
    n_groups = rows // GROUP
    n_chunks = rows // CHUNK
    per_group = GROUP // CHUNK

    @pl.when(pl.program_id(1) == 0)
    def _():
        s_ref[...] = jnp.zeros_like(s_ref)
        cq_ref[...] = jnp.zeros_like(cq_ref)
        ck_ref[...] = jnp.zeros_like(ck_ref)
        cv_ref[...] = jnp.zeros_like(cv_ref)

    q_all = _conv_silu(q_ref, cq_ref, wq_ref)
    k_all = _conv_silu(k_ref, ck_ref, wk_ref)
    v_all = _conv_silu(v_ref, cv_ref, wv_ref)
    hn = hn_ref[...]

    def lanes(x, i):
        return x[:, i * HEAD_DIM:(i + 1) * HEAD_DIM]

    qs = [_l2norm(lanes(q_all, j), HEAD_DIM ** -0.5) for j in range(GDN_KH_PER_STEP)]
    ks = [_l2norm(lanes(k_all, j)) for j in range(GDN_KH_PER_STEP)]

    row = lax.broadcasted_iota(jnp.int32, (GROUP, GROUP), 0)
    col = lax.broadcasted_iota(jnp.int32, (GROUP, GROUP), 1)
    same_chunk = (row // CHUNK) == (col // CHUNK)
    causal = (row >= col) & same_chunk
    strict = row > col

    heads = [(j, e) for j in range(GDN_KH_PER_STEP) for e in range(2)]
    head_lane = lax.broadcasted_iota(jnp.int32, gcol_ref.shape, 1)
    gcol_all = gcol_ref[...]
    bcol_all = bcol_ref[...]
    g_col, b_col, g_row = [], [], []
    for j, e in heads:
        head = 2 * (GDN_KH_PER_STEP * hb + j) + e
        sel = head_lane == head
        g_col.append(jnp.sum(jnp.where(sel, gcol_all, 0.0), axis=1, keepdims=True))
        b_col.append(jnp.sum(jnp.where(sel, bcol_all, 0.0), axis=1, keepdims=True))
        g_row.append(grow_ref[pl.ds(head, 1), :])

    def grp(x, g):
        return x[g * GROUP:(g + 1) * GROUP]

    kq = [[_mm_nt(jnp.concatenate([grp(qs[j], g), grp(ks[j], g)], axis=0), grp(ks[j], g))
           for g in range(n_groups)] for j in range(GDN_KH_PER_STEP)]

    items = [(h, g) for h in range(len(heads)) for g in range(n_groups)]
    a_list, aqk_list, rhs_list, qd_list, kd_list = [], [], [], [], []
    for h, g in items:
        j, e = heads[h]
        gc = grp(g_col[h], g)
        bc = grp(b_col[h], g)
        gr = g_row[h][:, g * GROUP:(g + 1) * GROUP]
        decay = jnp.exp2(jnp.where(causal, gc - gr, -jnp.inf))
        a_list.append(jnp.where(strict, bc * kq[j][g][GROUP:] * decay, 0.0))
        aqk_list.append(kq[j][g][:GROUP] * decay)
        eg = jnp.exp2(gc)
        kg = grp(ks[j], g)
        vg = grp(lanes(v_all, 2 * j + e), g)
        rhs_list.append(jnp.concatenate([vg * bc, kg * (bc * eg)], axis=1))
        g_last = jnp.concatenate(
            [jnp.broadcast_to(gc[(i + 1) * CHUNK - 1:(i + 1) * CHUNK], (CHUNK, 1)) for i in range(per_group)],
            axis=0)
        qd_list.append(grp(qs[j], g) * eg)
        kd_list.append(kg * jnp.exp2(g_last - gc))

    t_list = _unit_lower_inverses(a_list, row, col)
    uw_list = [_mm(t, r) for t, r in zip(t_list, rhs_list)]

    states = [s_ref[h] for h in range(len(heads))]
    for c in range(n_chunks):
        g, i = divmod(c, per_group)
        sl = slice(i * CHUNK, (i + 1) * CHUNK)
        idx = [h * n_groups + g for h in range(len(heads))]
        ws_qs = [_mm(jnp.concatenate([uw_list[n][sl, HEAD_DIM:], qd_list[n][sl]], axis=0), states[h])
                 for h, n in enumerate(idx)]
        v_new = [uw_list[n][sl, :HEAD_DIM] - r[:CHUNK] for n, r in zip(idx, ws_qs)]
        intra = [_mm(aqk_list[n][sl, i * CHUNK:(i + 1) * CHUNK], vn) for n, vn in zip(idx, v_new)]
        upd = [_mm_tn(kd_list[n][sl], vn) for n, vn in zip(idx, v_new)]
        for h in range(len(heads)):
            j, e = heads[h]
            g_last = g_col[h][(c + 1) * CHUNK - 1:(c + 1) * CHUNK]
            states[h] = states[h] * jnp.exp2(g_last) + upd[h]
            o = ws_qs[h][CHUNK:] + intra[h]
            z = z_ref[c * CHUNK:(c + 1) * CHUNK, (2 * j + e) * HEAD_DIM:(2 * j + e + 1) * HEAD_DIM].astype(F32)
            o_ref[c * CHUNK:(c + 1) * CHUNK, (2 * j + e) * HEAD_DIM:(2 * j + e + 1) * HEAD_DIM] = (
                _gated_rmsnorm(o, hn, z).astype(o_ref.dtype))
    for h in range(len(heads)):
        s_ref[h] = states[h]


def _gdn_mixer(proj, conv_w, g_row, g_col, b_col, head_norm):
    m = proj.shape[0]
    rows = GDN_ROWS
    nh = GDN_V_HEADS
    kw = GDN_KH_PER_STEP * HEAD_DIM
    vw = 2 * kw
    kb = GDN_KEY_DIM // kw
    vb = 2 * GDN_KEY_DIM // vw
    zb = GDN_CONV_DIM // vw
    return pl.pallas_call(
        _gdn_kernel,
        grid=(GDN_K_HEADS // GDN_KH_PER_STEP, m // rows),
        in_specs=[
            pl.BlockSpec((rows, kw), lambda h, t: (t, h)),
            pl.BlockSpec((rows, kw), lambda h, t: (t, kb + h)),
            pl.BlockSpec((rows, vw), lambda h, t: (t, vb + h)),
            pl.BlockSpec((rows, vw), lambda h, t: (t, zb + h)),
            pl.BlockSpec((GDN_CONV_TAPS, kw), lambda h, t: (0, h)),
            pl.BlockSpec((GDN_CONV_TAPS, kw), lambda h, t: (0, kb + h)),
            pl.BlockSpec((GDN_CONV_TAPS, vw), lambda h, t: (0, vb + h)),
            pl.BlockSpec((nh, rows), lambda h, t: (0, t)),
            pl.BlockSpec((rows, nh), lambda h, t: (t, 0)),
            pl.BlockSpec((rows, nh), lambda h, t: (t, 0)),
            pl.BlockSpec((1, HEAD_DIM), lambda h, t: (0, 0)),
        ],
        out_specs=pl.BlockSpec((rows, vw), lambda h, t: (t, h)),
        out_shape=jax.ShapeDtypeStruct((m, GDN_VAL_DIM), BF16),
        scratch_shapes=[
            pltpu.VMEM((2 * GDN_KH_PER_STEP, HEAD_DIM, HEAD_DIM), F32),
            pltpu.VMEM((CARRY_ROWS, kw), F32),
            pltpu.VMEM((CARRY_ROWS, kw), F32),
            pltpu.VMEM((CARRY_ROWS, vw), F32),
        ],
        compiler_params=_params("arbitrary", "arbitrary"),
        name="gdn_mixer",
    )(proj, proj, proj, proj, conv_w, conv_w, conv_w, g_row, g_col, b_col,
      head_norm.reshape(1, HEAD_DIM))


def _row_broadcast(x, first, period, rows):
    parts = [jnp.broadcast_to(x[r:r + 1], (period, x.shape[1])) for r in range(first, rows, period)]
    return jnp.concatenate(parts, axis=0)


def _hgrn_kernel(q_ref, f_ref, i_ref, g_ref, lb_ref, hn_ref, o_ref, s_ref):
    rows = q_ref.shape[0]
    n_groups = rows // GROUP
    n_chunks = rows // CHUNK
    per_group = GROUP // CHUNK

    @pl.when(pl.program_id(1) == 0)
    def _():
        s_ref[...] = jnp.zeros_like(s_ref)

    lbs = lb_ref[...]
    mx = jnp.max(lbs, axis=0, keepdims=True)
    ex = jnp.exp(lbs - mx)
    sm = ex / jnp.sum(ex, axis=0, keepdims=True)
    lb = (sm[0:1] + sm[1:2]) - sm[0:1]

    q = _silu(q_ref[...].astype(F32))
    fl = f_ref[...].astype(F32)
    v = i_ref[...].astype(F32)
    en = jnp.exp(-jnp.abs(fl))
    rc = 1.0 / (1.0 + en)
    pos = fl >= 0.0
    sig = jnp.where(pos, rc, en * rc)
    nsig = jnp.where(pos, en * rc, rc)
    logf = jnp.log(lb + (1.0 - lb) * sig)
    kk = (1.0 - lb) * nsig

    rowi = lax.broadcasted_iota(jnp.int32, (rows, HEAD_DIM), 0)
    row = lax.broadcasted_iota(jnp.int32, (GROUP, GROUP), 0)
    col = lax.broadcasted_iota(jnp.int32, (GROUP, GROUP), 1)

    tri = (((row // CHUNK) == (col // CHUNK)) & (row >= col)).astype(BF16)
    logf2 = logf * LOG2E
    hi = logf2.astype(BF16)
    rest = logf2 - hi.astype(F32)
    mid = rest.astype(BF16)
    lo = (rest - mid.astype(F32)).astype(BF16)
    pieces = jnp.concatenate([hi, mid, lo], axis=1)

    def mid_range(s):
        mid_row = (row // (2 * s)) * (2 * s) + (s - 1)
        return ((col > mid_row) & (col <= row)) | ((col > row) & (col <= mid_row))

    small_ranges = jnp.concatenate([mid_range(s).astype(BF16) for s in HGRN_SMALL_HALVES], axis=0)
    b_parts, small_parts = [], []
    for g in range(n_groups):
        pg = pieces[g * GROUP:(g + 1) * GROUP]
        s3 = jnp.dot(tri, pg, preferred_element_type=F32)
        b_parts.append(s3[:, :HEAD_DIM] + s3[:, HEAD_DIM:2 * HEAD_DIM] + s3[:, 2 * HEAD_DIM:])
        s2 = jnp.dot(small_ranges, pg[:, :2 * HEAD_DIM], preferred_element_type=F32)
        small_parts.append(s2[:, :HEAD_DIM] + s2[:, HEAD_DIM:])
    b = jnp.concatenate(b_parts, axis=0)
    b_last = _row_broadcast(b, CHUNK - 1, CHUNK, rows)
    qd = q * jnp.exp2(b)
    kd = kk * jnp.exp2(b_last - b)

    halves = HGRN_BIG_HALVES + HGRN_SMALL_HALVES
    level_x = []
    for s in HGRN_BIG_HALVES:
        b_mid = _row_broadcast(b, s - 1, 2 * s, rows)
        lower = (rowi % (2 * s)) >= s
        level_x.append(jnp.where(lower, q, kk) * jnp.exp2(-jnp.abs(b - b_mid)))
    for n, s in enumerate(HGRN_SMALL_HALVES):
        expo = jnp.concatenate([p[n * GROUP:(n + 1) * GROUP] for p in small_parts], axis=0)
        lower = (rowi % (2 * s)) >= s
        level_x.append(jnp.where(lower, q, kk) * jnp.exp2(expo))

    def rot8(x, d):
        return pltpu.roll(x.reshape(rows // 8, 8, HEAD_DIM), d, 1).reshape(rows, HEAD_DIM)

    diags = [jnp.sum(q * kk, axis=1, keepdims=True)]
    for d in range(1, HGRN_DIAG):
        e = jnp.exp2(jnp.minimum(b - rot8(b, d), 0.0))
        diags.append(jnp.sum(q * rot8(kk, d) * e, axis=1, keepdims=True))

    def grp(x, g):
        return x[g * GROUP:(g + 1) * GROUP]

    def chk(x, c):
        return x[c * CHUNK:(c + 1) * CHUNK]

    level_mm = [[_mm_nt(grp(x, g), grp(x, g)) for x in level_x] for g in range(n_groups)]
    kv = [_mm_tn(chk(v, c), chk(kd, c)) for c in range(n_chunks)]

    scores = []
    for g in range(n_groups):
        sc = jnp.zeros((GROUP, GROUP), F32)
        for s, mm in zip(halves, level_mm[g]):
            pair = ((row // (2 * s)) == (col // (2 * s))) & ((row % (2 * s)) >= s) & ((col % (2 * s)) < s)
            sc = jnp.where(pair, mm, sc)
        same_diag = (row // HGRN_DIAG) == (col // HGRN_DIAG)
        for d in range(HGRN_DIAG):
            sc = jnp.where(same_diag & (row - col == d), grp(diags[d], g), sc)
        scores.append(sc)

    state_t = s_ref[...]
    states = []
    for c in range(n_chunks):
        states.append(state_t)
        f_last = jnp.exp2(b[(c + 1) * CHUNK - 1:(c + 1) * CHUNK])
        state_t = state_t * f_last + kv[c]
    s_ref[...] = state_t

    intra = [_mm(scores[g], grp(v, g)) for g in range(n_groups)]
    inter = [_mm_nt(chk(qd, c), states[c]) for c in range(n_chunks)]
    hn = hn_ref[...]
    for c in range(n_chunks):
        g, i = divmod(c, per_group)
        o = inter[c] + intra[g][i * CHUNK:(i + 1) * CHUNK]
        gate = g_ref[c * CHUNK:(c + 1) * CHUNK, :].astype(F32)
        o_ref[c * CHUNK:(c + 1) * CHUNK, :] = _gated_rmsnorm(o, hn, gate).astype(o_ref.dtype)


def _hgrn_mixer(proj, lower_bounds, head_norm):
    m = proj.shape[0]
    rows = HGRN_ROWS
    nb = HGRN_DIM // HEAD_DIM
    return pl.pallas_call(
        _hgrn_kernel,
        grid=(HGRN_HEADS, m // rows),
        in_specs=[
            pl.BlockSpec((rows, HEAD_DIM), lambda h, t: (t, h)),
            pl.BlockSpec((rows, HEAD_DIM), lambda h, t: (t, nb + h)),
            pl.BlockSpec((rows, HEAD_DIM), lambda h, t: (t, 2 * nb + h)),
            pl.BlockSpec((rows, HEAD_DIM), lambda h, t: (t, 3 * nb + h)),
            pl.BlockSpec((2, HEAD_DIM), lambda h, t: (0, h)),
            pl.BlockSpec((1, HEAD_DIM), lambda h, t: (0, 0)),
        ],
        out_specs=pl.BlockSpec((rows, HEAD_DIM), lambda h, t: (t, h)),
        out_shape=jax.ShapeDtypeStruct((m, HGRN_DIM), BF16),
        scratch_shapes=[pltpu.VMEM((HEAD_DIM, HEAD_DIM), F32)],
        compiler_params=_params("arbitrary", "arbitrary"),
        name="hgrn_mixer",
    )(proj, proj, proj, proj, lower_bounds, head_norm.reshape(1, HEAD_DIM))


def kernel(x, gdn_norm, gdn_w_in, gdn_conv, gdn_a_log, gdn_dt_bias, gdn_head_norm, gdn_w_out,
           hgrn_norm, hgrn_w_in, hgrn_lower_bounds, hgrn_head_norm, hgrn_w_out,
           ffn_norm, ffn_w_gate, ffn_w_up, ffn_w_down, final_norm):
    assert x.shape == (1, SEQ, D_MODEL)
    h = x.reshape(SEQ, D_MODEL)

    w_in_t = jnp.swapaxes(gdn_w_in, 1, 2)
    w_ba_t = w_in_t[0, GDN_MAIN_DIM:]
    hn, g_row, b_row = _gdn_gates(h, gdn_norm[0], w_ba_t, gdn_a_log[0], gdn_dt_bias[0])
    proj = _proj(hn, w_in_t, 0, GDN_MAIN_DIM, tm=2048, tn=1024, transposed=True)
    mixed = _gdn_mixer(proj, gdn_conv[0], g_row, g_row.T, b_row.T, gdn_head_norm[0])
    h, hn = _proj_rows(mixed, gdn_w_out, 0, h, ffn_norm[0], 256, BF16)
    act = _swiglu_up(hn, ffn_w_gate, ffn_w_up, 0, tm=1024, tn=512)
    h, hn = _proj_rows(act, ffn_w_down, 0, h, hgrn_norm[0], 256, BF16)

    proj = _proj(hn, hgrn_w_in, 0, 4 * HGRN_DIM, tm=2048, tn=1024)
    mixed = _hgrn_mixer(proj, hgrn_lower_bounds, hgrn_head_norm[0])
    h, hn = _proj_rows(mixed, hgrn_w_out, 0, h, ffn_norm[1], 512, BF16)
    act = _swiglu_up(hn, ffn_w_gate, ffn_w_up, 1, tm=1024, tn=512)
    (out,) = _proj_rows(act, ffn_w_down, 1, h, final_norm, 256, F32, keep_sum=False)
    return out.reshape(1, SEQ, D_MODEL)
```

```python
import functools

import jax
import jax.numpy as jnp
from jax import lax
from jax.experimental import pallas as pl
from jax.experimental.pallas import tpu as pltpu

F32 = jnp.float32
BF16 = jnp.bfloat16

D_MODEL = 2048
SEQ = 8192
CHUNK = 64
HEAD_DIM = 128
EPS = 1e-6
LOG2E = 1.4426950408889634

GDN_K_HEADS = 16
GDN_V_HEADS = 32
GDN_KEY_DIM = GDN_K_HEADS * HEAD_DIM
GDN_VAL_DIM = GDN_V_HEADS * HEAD_DIM
GDN_CONV_DIM = 2 * GDN_KEY_DIM + GDN_VAL_DIM
GDN_MAIN_DIM = GDN_CONV_DIM + GDN_VAL_DIM
GDN_CONV_TAPS = 4

HGRN_HEADS = 16
HGRN_DIM = HGRN_HEADS * HEAD_DIM

VMEM_LIMIT_BYTES = 56 * 1024 * 1024
CARRY_ROWS = 8

GROUP = 2 * CHUNK
GDN_KH_PER_STEP = 16
GDN_ROWS = 128
HGRN_ROWS = 4096
HGRN_BIG_HALVES = (32, 16, 8)
HGRN_SMALL_HALVES = (4, 2)
HGRN_DIAG = 2
WEIGHT_CHUNK_ROWS = 512


def _params(*semantics):
    return pltpu.CompilerParams(dimension_semantics=semantics, vmem_limit_bytes=VMEM_LIMIT_BYTES)


def _mm(a, b):
    return jnp.dot(a.astype(BF16), b.astype(BF16), preferred_element_type=F32)


def _mm_nt(a, b):
    return lax.dot_general(a.astype(BF16), b.astype(BF16), (((1,), (1,)), ((), ())),
                           preferred_element_type=F32)


def _mm_tn(a, b):
    return lax.dot_general(a.astype(BF16), b.astype(BF16), (((0,), (0,)), ((), ())),
                           preferred_element_type=F32)


def _sigmoid(x):
    return 1.0 / (1.0 + jnp.exp(-x))


def _silu(x):
    h = 0.5 * x
    return h + h * jnp.tanh(h)


def _weight_spec(k, tn, layer, first_tile=0, transposed=False):
    if transposed:
        return pl.BlockSpec((None, tn, k), lambda j, i: (layer, first_tile + j, 0))
    return pl.BlockSpec((None, k, tn), lambda j, i: (layer, 0, first_tile + j))


def _load_weight(w_ref, wb_ref):
    w = w_ref[...]
    if w.shape != wb_ref.shape:
        w = w.T
    wb_ref[...] = w.astype(BF16)


def _proj_kernel(x_ref, w_ref, o_ref, wb_ref):
    @pl.when(pl.program_id(1) == 0)
    def _():
        _load_weight(w_ref, wb_ref)

    o_ref[...] = jnp.dot(x_ref[...], wb_ref[...], preferred_element_type=F32).astype(o_ref.dtype)


def _proj(x, w, layer, n_cols, tm, tn, first_col=0, transposed=False):
    m, k = x.shape
    return pl.pallas_call(
        _proj_kernel,
        grid=(n_cols // tn, m // tm),
        in_specs=[pl.BlockSpec((tm, k), lambda j, i: (i, 0)),
                  _weight_spec(k, tn, layer, first_col // tn, transposed)],
        out_specs=pl.BlockSpec((tm, tn), lambda j, i: (i, j)),
        out_shape=jax.ShapeDtypeStruct((m, n_cols), BF16),
        scratch_shapes=[pltpu.VMEM((k, tn), BF16)],
        compiler_params=_params("arbitrary", "arbitrary"),
        name="proj",
    )(x, w)


def _proj_rows_kernel(x_ref, w_ref, r_ref, nw_ref, *refs, n_load, keep_sum):
    wb_ref = refs[-1]
    s = pl.program_id(0)
    tk = w_ref.shape[0]

    @pl.when(s < n_load)
    def _():
        wb_ref[pl.ds(pl.multiple_of(s * tk, tk), tk), :] = w_ref[...].astype(BF16)

    @pl.when(s >= n_load)
    def _():
        h = r_ref[...] + jnp.dot(x_ref[...], wb_ref[...], preferred_element_type=F32)
        if keep_sum:
            refs[0][...] = h
        y = h * lax.rsqrt(jnp.mean(h * h, axis=-1, keepdims=True) + EPS)
        refs[-2][...] = (y * nw_ref[...]).astype(refs[-2].dtype)


def _proj_rows(x, w, layer, res, norm_w, tm, norm_dtype, keep_sum=True):
    m, k = x.shape
    n = w.shape[2]
    tk = WEIGHT_CHUNK_ROWS
    n_load = k // tk
    row = lambda s: (jnp.maximum(s - n_load, 0), 0)
    out_specs = [pl.BlockSpec((tm, n), row)]
    out_shape = [jax.ShapeDtypeStruct((m, n), norm_dtype)]
    if keep_sum:
        out_specs.insert(0, pl.BlockSpec((tm, n), row))
        out_shape.insert(0, jax.ShapeDtypeStruct((m, n), F32))
    return pl.pallas_call(
        functools.partial(_proj_rows_kernel, n_load=n_load, keep_sum=keep_sum),
        grid=(n_load + m // tm,),
        in_specs=[pl.BlockSpec((tm, k), row),
                  pl.BlockSpec((None, tk, n), lambda s: (layer, jnp.minimum(s, n_load - 1), 0)),
                  pl.BlockSpec((tm, n), row),
                  pl.BlockSpec((1, n), lambda s: (0, 0))],
        out_specs=out_specs,
        out_shape=out_shape,
        scratch_shapes=[pltpu.VMEM((k, n), BF16)],
        compiler_params=_params("arbitrary"),
        name="proj_rows",
    )(x, w, res, norm_w.reshape(1, n))


def _swiglu_up_kernel(x_ref, wg_ref, wu_ref, o_ref, wgb_ref, wub_ref):
    @pl.when(pl.program_id(1) == 0)
    def _():
        wgb_ref[...] = wg_ref[...].astype(BF16)
        wub_ref[...] = wu_ref[...].astype(BF16)

    x = x_ref[...]
    g = jnp.dot(x, wgb_ref[...], preferred_element_type=F32)
    u = jnp.dot(x, wub_ref[...], preferred_element_type=F32)
    o_ref[...] = (_silu(g) * u).astype(o_ref.dtype)


def _swiglu_up(x, w_gate, w_up, layer, tm, tn):
    m, k = x.shape
    n = w_gate.shape[2]
    return pl.pallas_call(
        _swiglu_up_kernel,
        grid=(n // tn, m // tm),
        in_specs=[pl.BlockSpec((tm, k), lambda j, i: (i, 0)), _weight_spec(k, tn, layer),
                  _weight_spec(k, tn, layer)],
        out_specs=pl.BlockSpec((tm, tn), lambda j, i: (i, j)),
        out_shape=jax.ShapeDtypeStruct((m, n), BF16),
        scratch_shapes=[pltpu.VMEM((k, tn), BF16), pltpu.VMEM((k, tn), BF16)],
        compiler_params=_params("arbitrary", "arbitrary"),
        name="swiglu_up",
    )(x, w_gate, w_up)


def _gdn_gate_kernel(x_ref, nw_ref, w_ref, alog_ref, dtb_ref, hn_ref, g_ref, beta_ref):
    x = x_ref[...]
    hn = (x * lax.rsqrt(jnp.mean(x * x, axis=-1, keepdims=True) + EPS) * nw_ref[...]).astype(BF16)
    hn_ref[...] = hn
    ba = _mm_nt(w_ref[...], hn)
    nh = GDN_V_HEADS
    beta_ref[...] = _sigmoid(ba[:nh])
    a = ba[nh:] + dtb_ref[...]
    softplus = jnp.maximum(a, 0.0) + jnp.log1p(jnp.exp(-jnp.abs(a)))
    g = (-LOG2E * jnp.exp(alog_ref[...])) * softplus
    pos = lax.broadcasted_iota(jnp.int32, g.shape, 1) % CHUNK
    shift = 1
    while shift < CHUNK:
        g = g + jnp.where(pos >= shift, pltpu.roll(g, shift, 1), 0.0)
        shift *= 2
    g_ref[...] = g


def _gdn_gates(x, norm_w, w_ba_t, a_log, dt_bias, rows=512):
    m, k = x.shape
    nh = GDN_V_HEADS
    return pl.pallas_call(
        _gdn_gate_kernel,
        grid=(m // rows,),
        in_specs=[pl.BlockSpec((rows, k), lambda i: (i, 0)),
                  pl.BlockSpec((1, k), lambda i: (0, 0)),
                  pl.BlockSpec((2 * nh, k), lambda i: (0, 0)),
                  pl.BlockSpec((nh, 1), lambda i: (0, 0)),
                  pl.BlockSpec((nh, 1), lambda i: (0, 0))],
        out_specs=[pl.BlockSpec((rows, k), lambda i: (i, 0)),
                   pl.BlockSpec((nh, rows), lambda i: (0, i)),
                   pl.BlockSpec((nh, rows), lambda i: (0, i))],
        out_shape=[jax.ShapeDtypeStruct((m, k), BF16),
                   jax.ShapeDtypeStruct((nh, m), F32), jax.ShapeDtypeStruct((nh, m), F32)],
        compiler_params=_params("arbitrary"),
        name="gdn_gates",
    )(x, norm_w.reshape(1, k), w_ba_t, a_log.reshape(nh, 1), dt_bias.reshape(nh, 1))


def _conv_silu(x_ref, carry_ref, w_ref):
    x = x_ref[...].astype(F32)
    rows, width = x.shape
    xe = jnp.concatenate([carry_ref[...], x], axis=0).reshape(rows // 8 + 1, 8, width)
    in_tile = lax.broadcasted_iota(jnp.int32, (1, 8, width), 1)
    w = w_ref[...]
    taps = GDN_CONV_TAPS
    y = w[taps - 1:taps] * x
    for j in range(1, taps):
        r = pltpu.roll(xe, j, 1)
        shifted = jnp.where(in_tile < j, r[:-1], r[1:]).reshape(rows, width)
        y = y + w[taps - 1 - j:taps - j] * shifted
    carry_ref[...] = x[rows - CARRY_ROWS:]
    return _silu(y)


def _l2norm(x, scale=1.0):
    return x * (lax.rsqrt(jnp.sum(x * x, axis=-1, keepdims=True) + 1e-6) * scale)


def _gated_rmsnorm(o, w, z):
    y = o * lax.rsqrt(jnp.mean(o * o, axis=-1, keepdims=True) + EPS)
    return y * w * _silu(z)


def _unit_lower_inverses(a_list, row, col):
    eye = (row == col).astype(F32)
    same16 = (row // 16) == (col // 16)
    same32 = (row // 32) == (col // 32)
    off32 = same32 & jnp.logical_not(same16)
    off64 = jnp.logical_not(same32)
    ad = [jnp.where(same16, a, 0.0) for a in a_list]
    n = row.shape[0]
    p = [_mm(x, x) for x in ad]
    t = [eye - x for x in ad]
    for _ in range(2):
        both = [_mm(jnp.concatenate([x, y], axis=0), y) for x, y in zip(t, p)]
        t = [x + b[:n] for x, b in zip(t, both)]
        p = [b[n:] for b in both]
    t = [x + _mm(x, y) for x, y in zip(t, p)]
    for mask in (off32, off64):
        y = [_mm(jnp.where(mask, a, 0.0), x) for a, x in zip(a_list, t)]
        t = [x - _mm(x, yy) for x, yy in zip(t, y)]
    return t


def _gdn_kernel(q_ref, k_ref, v_ref, z_ref, wq_ref, wk_ref, wv_ref, grow_ref,
                gcol_ref, bcol_ref, hn_ref, o_ref, s_ref, cq_ref, ck_ref, cv_ref):
    hb = pl.program_id(0)
    rows = q_ref.shape[0]
    n_groups = rows // GROUP
    n_chunks = rows // CHUNK
    per_group = GROUP // CHUNK

    @pl.when(pl.program_id(1) == 0)
    def _():
        s_ref[...] = jnp.zeros_like(s_ref)
        cq_ref[...] = jnp.zeros_like(cq_ref)
        ck_ref[...] = jnp.zeros_like(ck_ref)
        cv_ref[...] = jnp.zeros_like(cv_ref)

    q_all = _conv_silu(q_ref, cq_ref, wq_ref)
    k_all = _conv_silu(k_ref, ck_ref, wk_ref)
    v_all = _conv_silu(v_ref, cv_ref, wv_ref)
    hn = hn_ref[...]

    def lanes(x, i):
        return x[:, i * HEAD_DIM:(i + 1) * HEAD_DIM]

    qs = [_l2norm(lanes(q_all, j), HEAD_DIM ** -0.5) for j in range(GDN_KH_PER_STEP)]
    ks = [_l2norm(lanes(k_all, j)) for j in range(GDN_KH_PER_STEP)]

    row = lax.broadcasted_iota(jnp.int32, (GROUP, GROUP), 0)
    col = lax.broadcasted_iota(jnp.int32, (GROUP, GROUP), 1)
    same_chunk = (row // CHUNK) == (col // CHUNK)
    causal = (row >= col) & same_chunk
    strict = row > col

    heads = [(j, e) for j in range(GDN_KH_PER_STEP) for e in range(2)]
    head_lane = lax.broadcasted_iota(jnp.int32, gcol_ref.shape, 1)
    gcol_all = gcol_ref[...]
    bcol_all = bcol_ref[...]
    g_col, b_col, g_row = [], [], []
    for j, e in heads:
        head = 2 * (GDN_KH_PER_STEP * hb + j) + e
        sel = head_lane == head
        g_col.append(jnp.sum(jnp.where(sel, gcol_all, 0.0), axis=1, keepdims=True))
        b_col.append(jnp.sum(jnp.where(sel, bcol_all, 0.0), axis=1, keepdims=True))
        g_row.append(grow_ref[pl.ds(head, 1), :])

    def grp(x, g):
        return x[g * GROUP:(g + 1) * GROUP]

    kq = [[_mm_nt(jnp.concatenate([grp(qs[j], g), grp(ks[j], g)], axis=0), grp(ks[j], g))
           for g in range(n_groups)] for j in range(GDN_KH_PER_STEP)]

    items = [(h, g) for h in range(len(heads)) for g in range(n_groups)]
    a_list, aqk_list, rhs_list, qd_list, kd_list = [], [], [], [], []
    for h, g in items:
        j, e = heads[h]
        gc = grp(g_col[h], g)
        bc = grp(b_col[h], g)
        gr = g_row[h][:, g * GROUP:(g + 1) * GROUP]
        decay = jnp.exp2(jnp.where(causal, gc - gr, -jnp.inf))
        a_list.append(jnp.where(strict, bc * kq[j][g][GROUP:] * decay, 0.0))
        aqk_list.append(kq[j][g][:GROUP] * decay)
        eg = jnp.exp2(gc)
        kg = grp(ks[j], g)
        vg = grp(lanes(v_all, 2 * j + e), g)
        rhs_list.append(jnp.concatenate([vg * bc, kg * (bc * eg)], axis=1))
        g_last = jnp.concatenate(
            [jnp.broadcast_to(gc[(i + 1) * CHUNK - 1:(i + 1) * CHUNK], (CHUNK, 1)) for i in range(per_group)],
            axis=0)
        qd_list.append(grp(qs[j], g) * eg)
        kd_list.append(kg * jnp.exp2(g_last - gc))

    t_list = _unit_lower_inverses(a_list, row, col)
    uw_list = [_mm(t, r) for t, r in zip(t_list, rhs_list)]

    states = [s_ref[h] for h in range(len(heads))]
    for c in range(n_chunks):
        g, i = divmod(c, per_group)
        sl = slice(i * CHUNK, (i + 1) * CHUNK)
        idx = [h * n_groups + g for h in range(len(heads))]
        ws_qs = [_mm(jnp.concatenate([uw_list[n][sl, HEAD_DIM:], qd_list[n][sl]], axis=0), states[h])
                 for h, n in enumerate(idx)]
        v_new = [uw_list[n][sl, :HEAD_DIM] - r[:CHUNK] for n, r in zip(idx, ws_qs)]
        intra = [_mm(aqk_list[n][sl, i * CHUNK:(i + 1) * CHUNK], vn) for n, vn in zip(idx, v_new)]
        upd = [_mm_tn(kd_list[n][sl], vn) for n, vn in zip(idx, v_new)]
        for h in range(len(heads)):
            j, e = heads[h]
            g_last = g_col[h][(c + 1) * CHUNK - 1:(c + 1) * CHUNK]
            states[h] = states[h] * jnp.exp2(g_last) + upd[h]
            o = ws_qs[h][CHUNK:] + intra[h]
            z = z_ref[c * CHUNK:(c + 1) * CHUNK, (2 * j + e) * HEAD_DIM:(2 * j + e + 1) * HEAD_DIM].astype(F32)
            o_ref[c * CHUNK:(c + 1) * CHUNK, (2 * j + e) * HEAD_DIM:(2 * j + e + 1) * HEAD_DIM] = (
                _gated_rmsnorm(o, hn, z).astype(o_ref.dtype))
    for h in range(len(heads)):
        s_ref[h] = states[h]


def _gdn_mixer(proj, conv_w, g_row, g_col, b_col, head_norm):
    m = proj.shape[0]
    rows = GDN_ROWS
    nh = GDN_V_HEADS
    kw = GDN_KH_PER_STEP * HEAD_DIM
    vw = 2 * kw
    kb = GDN_KEY_DIM // kw
    vb = 2 * GDN_KEY_DIM // vw
    zb = GDN_CONV_DIM // vw
    return pl.pallas_call(
        _gdn_kernel,
        grid=(GDN_K_HEADS // GDN_KH_PER_STEP, m // rows),
        in_specs=[
            pl.BlockSpec((rows, kw), lambda h, t: (t, h)),
            pl.BlockSpec((rows, kw), lambda h, t: (t, kb + h)),
            pl.BlockSpec((rows, vw), lambda h, t: (t, vb + h)),
            pl.BlockSpec((rows, vw), lambda h, t: (t, zb + h)),
            pl.BlockSpec((GDN_CONV_TAPS, kw), lambda h, t: (0, h)),
            pl.BlockSpec((GDN_CONV_TAPS, kw), lambda h, t: (0, kb + h)),
            pl.BlockSpec((GDN_CONV_TAPS, vw), lambda h, t: (0, vb + h)),
            pl.BlockSpec((nh, rows), lambda h, t: (0, t)),
            pl.BlockSpec((rows, nh), lambda h, t: (t, 0)),
            pl.BlockSpec((rows, nh), lambda h, t: (t, 0)),
            pl.BlockSpec((1, HEAD_DIM), lambda h, t: (0, 0)),
        ],
        out_specs=pl.BlockSpec((rows, vw), lambda h, t: (t, h)),
        out_shape=jax.ShapeDtypeStruct((m, GDN_VAL_DIM), BF16),
        scratch_shapes=[
            pltpu.VMEM((2 * GDN_KH_PER_STEP, HEAD_DIM, HEAD_DIM), F32),
            pltpu.VMEM((CARRY_ROWS, kw), F32),
            pltpu.VMEM((CARRY_ROWS, kw), F32),
            pltpu.VMEM((CARRY_ROWS, vw), F32),
        ],
        compiler_params=_params("arbitrary", "arbitrary"),
        name="gdn_mixer",
    )(proj, proj, proj, proj, conv_w, conv_w, conv_w, g_row, g_col, b_col,
      head_norm.reshape(1, HEAD_DIM))


def _row_broadcast(x, first, period, rows):
    parts = [jnp.broadcast_to(x[r:r + 1], (period, x.shape[1])) for r in range(first, rows, period)]
    return jnp.concatenate(parts, axis=0)


def _hgrn_kernel(q_ref, f_ref, i_ref, g_ref, lb_ref, hn_ref, o_ref, s_ref):
    rows = q_ref.shape[0]
    n_groups = rows // GROUP
    n_chunks = rows // CHUNK
    per_group = GROUP // CHUNK

    @pl.when(pl.program_id(1) == 0)
    def _():
        s_ref[...] = jnp.zeros_like(s_ref)

    lbs = lb_ref[...]
    mx = jnp.max(lbs, axis=0, keepdims=True)
    ex = jnp.exp(lbs - mx)
    sm = ex / jnp.sum(ex, axis=0, keepdims=True)
    lb = (sm[0:1] + sm[1:2]) - sm[0:1]

    q = _silu(q_ref[...].astype(F32))
    fl = f_ref[...].astype(F32)
    v = i_ref[...].astype(F32)
    en = jnp.exp(-jnp.abs(fl))
    rc = 1.0 / (1.0 + en)
    pos = fl >= 0.0
    sig = jnp.where(pos, rc, en * rc)
    nsig = jnp.where(pos, en * rc, rc)
    logf = jnp.log(lb + (1.0 - lb) * sig)
    kk = (1.0 - lb) * nsig

    rowi = lax.broadcasted_iota(jnp.int32, (rows, HEAD_DIM), 0)
    row = lax.broadcasted_iota(jnp.int32, (GROUP, GROUP), 0)
    col = lax.broadcasted_iota(jnp.int32, (GROUP, GROUP), 1)

    tri = (((row // CHUNK) == (col // CHUNK)) & (row >= col)).astype(BF16)
    logf2 = logf * LOG2E
    hi = logf2.astype(BF16)
    rest = logf2 - hi.astype(F32)
    mid = rest.astype(BF16)
    lo = (rest - mid.astype(F32)).astype(BF16)
    pieces = jnp.concatenate([hi, mid, lo], axis=1)

    def mid_range(s):
        mid_row = (row // (2 * s)) * (2 * s) + (s - 1)
        return ((col > mid_row) & (col <= row)) | ((col > row) & (col <= mid_row))

    small_ranges = jnp.concatenate([mid_range(s).astype(BF16) for s in HGRN_SMALL_HALVES], axis=0)
    b_parts, small_parts = [], []
    for g in range(n_groups):
        pg = pieces[g * GROUP:(g + 1) * GROUP]
        s3 = jnp.dot(tri, pg, preferred_element_type=F32)
        b_parts.append(s3[:, :HEAD_DIM] + s3[:, HEAD_DIM:2 * HEAD_DIM] + s3[:, 2 * HEAD_DIM:])
        s2 = jnp.dot(small_ranges, pg[:, :2 * HEAD_DIM], preferred_element_type=F32)
        small_parts.append(s2[:, :HEAD_DIM] + s2[:, HEAD_DIM:])
    b = jnp.concatenate(b_parts, axis=0)
    b_last = _row_broadcast(b, CHUNK - 1, CHUNK, rows)
    qd = q * jnp.exp2(b)
    kd = kk * jnp.exp2(b_last - b)

    halves = HGRN_BIG_HALVES + HGRN_SMALL_HALVES
    level_x = []
    for s in HGRN_BIG_HALVES:
        b_mid = _row_broadcast(b, s - 1, 2 * s, rows)
        lower = (rowi % (2 * s)) >= s
        level_x.append(jnp.where(lower, q, kk) * jnp.exp2(-jnp.abs(b - b_mid)))
    for n, s in enumerate(HGRN_SMALL_HALVES):
        expo = jnp.concatenate([p[n * GROUP:(n + 1) * GROUP] for p in small_parts], axis=0)
        lower = (rowi % (2 * s)) >= s
        level_x.append(jnp.where(lower, q, kk) * jnp.exp2(expo))

    def rot8(x, d):
        return pltpu.roll(x.reshape(rows // 8, 8, HEAD_DIM), d, 1).reshape(rows, HEAD_DIM)

    diags = [jnp.sum(q * kk, axis=1, keepdims=True)]
    for d in range(1, HGRN_DIAG):
        e = jnp.exp2(jnp.minimum(b - rot8(b, d), 0.0))
        diags.append(jnp.sum(q * rot8(kk, d) * e, axis=1, keepdims=True))

    def grp(x, g):
        return x[g * GROUP:(g + 1) * GROUP]

    def chk(x, c):
        return x[c * CHUNK:(c + 1) * CHUNK]

    level_mm = [[_mm_nt(grp(x, g), grp(x, g)) for x in level_x] for g in range(n_groups)]
    kv = [_mm_tn(chk(v, c), chk(kd, c)) for c in range(n_chunks)]

    scores = []
    for g in range(n_groups):
        sc = jnp.zeros((GROUP, GROUP), F32)
        for s, mm in zip(halves, level_mm[g]):
            pair = ((row // (2 * s)) == (col // (2 * s))) & ((row % (2 * s)) >= s) & ((col % (2 * s)) < s)
            sc = jnp.where(pair, mm, sc)
        same_diag = (row // HGRN_DIAG) == (col // HGRN_DIAG)
        for d in range(HGRN_DIAG):
            sc = jnp.where(same_diag & (row - col == d), grp(diags[d], g), sc)
        scores.append(sc)

    state_t = s_ref[...]
    states = []
    for c in range(n_chunks):
        states.append(state_t)
        f_last = jnp.exp2(b[(c + 1) * CHUNK - 1:(c + 1) * CHUNK])
        state_t = state_t * f_last + kv[c]
    s_ref[...] = state_t

    intra = [_mm(scores[g], grp(v, g)) for g in range(n_groups)]
    inter = [_mm_nt(chk(qd, c), states[c]) for c in range(n_chunks)]
    hn = hn_ref[...]
    for c in range(n_chunks):
        g, i = divmod(c, per_group)
        o = inter[c] + intra[g][i * CHUNK:(i + 1) * CHUNK]
        gate = g_ref[c * CHUNK:(c + 1) * CHUNK, :].astype(F32)
        o_ref[c * CHUNK:(c + 1) * CHUNK, :] = _gated_rmsnorm(o, hn, gate).astype(o_ref.dtype)


def _hgrn_mixer(proj, lower_bounds, head_norm):
    m = proj.shape[0]
    rows = HGRN_ROWS
    nb = HGRN_DIM // HEAD_DIM
    return pl.pallas_call(
        _hgrn_kernel,
        grid=(HGRN_HEADS, m // rows),
        in_specs=[
            pl.BlockSpec((rows, HEAD_DIM), lambda h, t: (t, h)),
            pl.BlockSpec((rows, HEAD_DIM), lambda h, t: (t, nb + h)),
            pl.BlockSpec((rows, HEAD_DIM), lambda h, t: (t, 2 * nb + h)),
            pl.BlockSpec((rows, HEAD_DIM), lambda h, t: (t, 3 * nb + h)),
            pl.BlockSpec((2, HEAD_DIM), lambda h, t: (0, h)),
            pl.BlockSpec((1, HEAD_DIM), lambda h, t: (0, 0)),
        ],
        out_specs=pl.BlockSpec((rows, HEAD_DIM), lambda h, t: (t, h)),
        out_shape=jax.ShapeDtypeStruct((m, HGRN_DIM), BF16),
        scratch_shapes=[pltpu.VMEM((HEAD_DIM, HEAD_DIM), F32)],
        compiler_params=_params("arbitrary", "arbitrary"),
        name="hgrn_mixer",
    )(proj, proj, proj, proj, lower_bounds, head_norm.reshape(1, HEAD_DIM))


def kernel(x, gdn_norm, gdn_w_in, gdn_conv, gdn_a_log, gdn_dt_bias, gdn_head_norm, gdn_w_out,
           hgrn_norm, hgrn_w_in, hgrn_lower_bounds, hgrn_head_norm, hgrn_w_out,
           ffn_norm, ffn_w_gate, ffn_w_up, ffn_w_down, final_norm):
    assert x.shape == (1, SEQ, D_MODEL)
    h = x.reshape(SEQ, D_MODEL)

    w_in_t = jnp.swapaxes(gdn_w_in, 1, 2)
    w_ba_t = w_in_t[0, GDN_MAIN_DIM:]
    hn, g_row, b_row = _gdn_gates(h, gdn_norm[0], w_ba_t, gdn_a_log[0], gdn_dt_bias[0])
    proj = _proj(hn, w_in_t, 0, GDN_MAIN_DIM, tm=2048, tn=1024, transposed=True)
    mixed = _gdn_mixer(proj, gdn_conv[0], g_row, g_row.T, b_row.T, gdn_head_norm[0])
    h, hn = _proj_rows(mixed, gdn_w_out, 0, h, ffn_norm[0], 512, BF16)
    act = _swiglu_up(hn, ffn_w_gate, ffn_w_up, 0, tm=1024, tn=512)
    h, hn = _proj_rows(act, ffn_w_down, 0, h, hgrn_norm[0], 256, BF16)

    proj = _proj(hn, hgrn_w_in, 0, 4 * HGRN_DIM, tm=2048, tn=1024)
    mixed = _hgrn_mixer(proj, hgrn_lower_bounds, hgrn_head_norm[0])
    h, hn = _proj_rows(mixed, hgrn_w_out, 0, h, ffn_norm[1], 512, BF16)
    act = _swiglu_up(hn, ffn_w_gate, ffn_w_up, 1, tm=1024, tn=512)
    (out,) = _proj_rows(act, ffn_w_down, 1, h, final_norm, 256, F32, keep_sum=False)
    return out.reshape(1, SEQ, D_MODEL)
```
